```python
import jax, jax.numpy as jnp
from jax import lax
import numpy as np

D_MODEL = 2048
BATCH = 4
SEQ = 2048
DEPTH = 4

GRID_W = 64
CTX_LEN = 256
N_MIXERS = 2
N_A_LAYERS = (DEPTH + 1) // 2
N_B_LAYERS = DEPTH // 2
A_CHUNK = 128
A_GROUP_DIM = 128
A_GROUPS = D_MODEL // A_GROUP_DIM
HEAD_DIM = 64
N_Q_HEADS = D_MODEL // HEAD_DIM
KV_GROUP = 8
N_KV_HEADS = N_Q_HEADS // KV_GROUP
WINDOW = 128
ATT_BLOCK = 128
ROPE_THETA = 10000.0
FFN_HIDDEN = -(-(8 * D_MODEL // 3) // 256) * 256
N_MOD = 6
EPS = 1e-6
NEG_INF = -1e30

kernel_name = "hybrid_gmlp_swa_diffusion_trunk"


def rmsnorm(x, g):
    x32 = x.astype(jnp.float32)
    y = x32 * lax.rsqrt(jnp.mean(x32 * x32, axis=-1, keepdims=True) + EPS)
    return y.astype(x.dtype) * g


def layernorm(x, g):
    x32 = x.astype(jnp.float32)
    mu = jnp.mean(x32, axis=-1, keepdims=True)
    var = jnp.mean(jnp.square(x32 - mu), axis=-1, keepdims=True)
    return ((x32 - mu) * lax.rsqrt(var + EPS)).astype(x.dtype) * g


def adaln(h, shift, scale):
    return h * (1 + scale) + shift


def axial_rope_tables(seq_len, dtype):
    rows = seq_len // GRID_W
    row = jnp.repeat(jnp.arange(rows), GRID_W)
    col = jnp.tile(jnp.arange(GRID_W), rows)
    pos = jnp.stack([row, col], axis=-1).astype(jnp.float32)
    n_freq = HEAD_DIM // 4
    inv = ROPE_THETA ** (-jnp.arange(n_freq, dtype=jnp.float32) / n_freq)
    ang = pos[:, :, None] * inv
    return jnp.cos(ang).astype(dtype), jnp.sin(ang).astype(dtype)


def apply_axial_rope(x, cos, sin):
    shp = x.shape
    xr = x.reshape(shp[0], shp[1], -1, 2, 2, HEAD_DIM // 4)
    x1, x2 = xr[..., 0, :], xr[..., 1, :]
    c = cos[None, :, None]
    s = sin[None, :, None]
    out = jnp.stack([x1 * c - x2 * s, x2 * c + x1 * s], axis=-2)
    return out.reshape(shp)


def chunked_spatial_gating(h, w_in, v_g, w_s, b_s, w_out):
    B, L, _ = h.shape
    z = jax.nn.gelu(h @ w_in, approximate=False)
    u, v = jnp.split(z, 2, axis=-1)
    v = layernorm(v, v_g).reshape(B, L // A_CHUNK, A_CHUNK, A_GROUPS, A_GROUP_DIM)
    s = jnp.einsum('gpq,bnqgc->bnpgc', w_s, v) + b_s.T[:, :, None]
    return (u * s.reshape(B, L, D_MODEL)) @ w_out


def split_qkv(h, w_qkv):
    B, L, _ = h.shape
    q, k, v = jnp.split(h @ w_qkv, [N_Q_HEADS * HEAD_DIM, (N_Q_HEADS + N_KV_HEADS) * HEAD_DIM], axis=-1)
    return (q.reshape(B, L, N_KV_HEADS, KV_GROUP, HEAD_DIM),
            k.reshape(B, L, N_KV_HEADS, HEAD_DIM),
            v.reshape(B, L, N_KV_HEADS, HEAD_DIM))


def sink_softmax(logits, sink):
    sk = jnp.broadcast_to(sink.astype(jnp.float32)[None, :, :, None, None], logits.shape[:-1] + (1,))
    p = jax.nn.softmax(jnp.concatenate([logits, sk], axis=-1), axis=-1)
    return p[..., :-1]


def windowed_gqa(h_lat, h_ctx, w_qkv, sink, w_o, cos, sin, need_ctx):
    B, S, _ = h_lat.shape
    scale = HEAD_DIM ** -0.5
    sink = sink.reshape(N_KV_HEADS, KV_GROUP)
    q, k, v = split_qkv(h_lat, w_qkv)
    q = apply_axial_rope(q, cos, sin)
    k = apply_axial_rope(k, cos, sin)
    qc, kc, vc = split_qkv(h_ctx, w_qkv)
    n_ctx = kc.shape[1]
    nb = S // ATT_BLOCK
    band = 3 * ATT_BLOCK
    q_blocks = jnp.moveaxis(q.reshape(B, nb, ATT_BLOCK, N_KV_HEADS, KV_GROUP, HEAD_DIM), 1, 0)
    pad = ((0, 0), (ATT_BLOCK, ATT_BLOCK), (0, 0), (0, 0))
    k_pad = jnp.pad(k, pad)
    v_pad = jnp.pad(v, pad)
    qi = jnp.arange(ATT_BLOCK)[:, None]
    kj = jnp.arange(band)[None, :]

    def block(args):
        n, qb = args
        start = n * ATT_BLOCK
        kb = lax.dynamic_slice_in_dim(k_pad, start, band, axis=1)
        vb = lax.dynamic_slice_in_dim(v_pad, start, band, axis=1)
        kpos = start - ATT_BLOCK + kj
        qpos = start + qi
        valid = (kpos >= 0) & (kpos < S) & (jnp.abs(qpos - kpos) <= WINDOW)
        s_band = jnp.einsum('bqhgd,bkhd->bhgqk', qb, kb).astype(jnp.float32) * scale
        s_band = jnp.where(valid, s_band, NEG_INF)
        s_ctx = jnp.einsum('bqhgd,bkhd->bhgqk', qb, kc).astype(jnp.float32) * scale
        p = sink_softmax(jnp.concatenate([s_ctx, s_band], axis=-1), sink).astype(vb.dtype)
        return (jnp.einsum('bhgqk,bkhd->bqhgd', p[..., :n_ctx], vc)
                + jnp.einsum('bhgqk,bkhd->bqhgd', p[..., n_ctx:], vb))

    o = lax.map(block, (jnp.arange(nb), q_blocks))
    o_lat = jnp.moveaxis(o, 0, 1).reshape(B, S, D_MODEL) @ w_o
    o_ctx = None
    if need_ctx:
        s = jnp.einsum('bqhgd,bkhd->bhgqk', qc, kc).astype(jnp.float32) * scale
        p = sink_softmax(s, sink).astype(vc.dtype)
        o_ctx = jnp.einsum('bhgqk,bkhd->bqhgd', p, vc).reshape(B, n_ctx, D_MODEL) @ w_o
    return o_lat, o_ctx


def swiglu(h, w_gu, w_down):
    g, u = jnp.split(h @ w_gu, 2, axis=-1)
    return (jax.nn.silu(g) * u) @ w_down


def setup_inputs(seed: int = 0) -> dict:
    key = jax.random.key(seed)
    ks = jax.random.split(key, 18)
    f32 = jnp.float32
    D = D_MODEL
    nrm = lambda k, shape, fan_in: jax.random.normal(k, shape, f32) * (fan_in ** -0.5)
    return {
        "x": jax.random.normal(ks[0], (BATCH, SEQ, D), f32),
        "c": jax.random.normal(ks[1], (BATCH, D), f32),
        "ctx": jax.random.normal(ks[2], (BATCH, CTX_LEN, D), f32),
        "c_ctx": jax.random.normal(ks[3], (D,), f32),
        "w_mod": nrm(ks[4], (DEPTH, D, N_MOD * D), D),
        "b_mod": 0.02 * jax.random.normal(ks[5], (DEPTH, N_MOD * D), f32),
        "norm_g": 1.0 + 0.05 * jax.random.normal(ks[6], (DEPTH, 4, D), f32),
        "a_w_in": nrm(ks[7], (N_A_LAYERS, D, 2 * D), D),
        "a_v_norm_g": 1.0 + 0.05 * jax.random.normal(ks[8], (N_A_LAYERS, D), f32),
        "a_w_s": nrm(ks[9], (N_A_LAYERS, A_GROUPS, A_CHUNK, A_CHUNK), A_CHUNK),
        "a_b_s": 1.0 + 0.05 * jax.random.normal(ks[10], (N_A_LAYERS, A_GROUPS, A_CHUNK), f32),
        "a_w_out": nrm(ks[11], (N_A_LAYERS, D, D), D),
        "b_w_qkv": nrm(ks[12], (N_B_LAYERS, D, (N_Q_HEADS + 2 * N_KV_HEADS) * HEAD_DIM), D),
        "b_sink": jax.random.normal(ks[13], (N_B_LAYERS, N_Q_HEADS), f32),
        "b_w_o": nrm(ks[14], (N_B_LAYERS, N_Q_HEADS * HEAD_DIM, D), N_Q_HEADS * HEAD_DIM),
        "f_w_gu": nrm(ks[15], (DEPTH, D, 2 * FFN_HIDDEN), D),
        "f_w_down": nrm(ks[16], (DEPTH, FFN_HIDDEN, D), FFN_HIDDEN),
    }


def reference(x, c, ctx, c_ctx, w_mod, b_mod, norm_g, a_w_in, a_v_norm_g, a_w_s, a_b_s, a_w_out,
              b_w_qkv, b_sink, b_w_o, f_w_gu, f_w_down):
    B, S, _ = x.shape
    cos, sin = axial_rope_tables(S, x.dtype)
    silu_c = jax.nn.silu(c)
    silu_cc = jax.nn.silu(c_ctx)
    for i in range(DEPTH):
        last = i == DEPTH - 1
        is_attn = (i % N_MIXERS) == 1
        j = i // N_MIXERS
        need_ctx_in = is_attn or not last
        mod = (silu_c @ w_mod[i] + b_mod[i]).reshape(B, N_MOD, 1, D_MODEL)
        sh_m, sc_m, g_m, sh_f, sc_f, g_f = (mod[:, t] for t in range(N_MOD))
        h_lat = adaln(rmsnorm(x, norm_g[i, 0]), sh_m, sc_m)
        if need_ctx_in:
            mod_c = (silu_cc @ w_mod[i] + b_mod[i]).reshape(N_MOD, D_MODEL)
            h_ctx = adaln(rmsnorm(ctx, norm_g[i, 0]), mod_c[0], mod_c[1])
        if is_attn:
            y_lat, y_ctx = windowed_gqa(h_lat, h_ctx, b_w_qkv[j], b_sink[j], b_w_o[j], cos, sin,
                                        need_ctx=not last)
        else:
            y_lat = chunked_spatial_gating(h_lat, a_w_in[j], a_v_norm_g[j], a_w_s[j], a_b_s[j], a_w_out[j])
            y_ctx = None
            if not last:
                y_ctx = chunked_spatial_gating(h_ctx, a_w_in[j], a_v_norm_g[j], a_w_s[j], a_b_s[j], a_w_out[j])
        x = x + g_m * rmsnorm(y_lat, norm_g[i, 1])
        if not last:
            ctx = ctx + mod_c[2] * rmsnorm(y_ctx, norm_g[i, 1])
        f_lat = swiglu(adaln(rmsnorm(x, norm_g[i, 2]), sh_f, sc_f), f_w_gu[i], f_w_down[i])
        x = x + g_f * rmsnorm(f_lat, norm_g[i, 3])
        if not last:
            f_ctx = swiglu(adaln(rmsnorm(ctx, norm_g[i, 2]), mod_c[3], mod_c[4]), f_w_gu[i], f_w_down[i])
            ctx = ctx + mod_c[5] * rmsnorm(f_ctx, norm_g[i, 3])
    return x
```

```python
import functools

import jax
import jax.numpy as jnp
import numpy as np
from jax import lax
from jax.experimental import pallas as pl
from jax.experimental.pallas import tpu as pltpu

D = 2048
BATCH = 4
SEQ = 2048
DEPTH = 4
GRID_W = 64
CTX_LEN = 256
CHUNK = 128
GROUP_DIM = 128
GROUPS = D // GROUP_DIM
HEAD_DIM = 64
N_Q_HEADS = D // HEAD_DIM
KV_GROUP = 8
N_KV_HEADS = N_Q_HEADS // KV_GROUP
WINDOW = 128
ATT_BLOCK = 128
BAND = 3 * ATT_BLOCK
ROPE_THETA = 10000.0
FFN_HIDDEN = 5632
N_MOD = 6
EPS = 1e-6
NEG_INF = -1e30

T_LAT = BATCH * SEQ
T_CTX = BATCH * CTX_LEN
T_ALL = T_LAT + T_CTX
MOD_ROWS = 8
LANES = 128
PAIR = 2 * HEAD_DIM
KV_EXT = N_KV_HEADS * 2 * PAIR

TM_MIX = 256
TM_FFN = 512
TH_FFN = 512
TN_MOD = 1024
VMEM_LIMIT = 56 * 1024 * 1024

bf16 = jnp.bfloat16
f32 = jnp.float32


def _rms(x, g):
    return x * lax.rsqrt(jnp.mean(x * x, axis=-1, keepdims=True) + EPS) * g


def _dot(a, b):
    return jnp.dot(a, b, preferred_element_type=f32)


def _dot_nt(a, b):
    return lax.dot_general(a, b, (((1,), (1,)), ((), ())), preferred_element_type=f32)


def _mod_row(tm):
    return lambda t: jnp.minimum((t * tm) // SEQ, BATCH)


def _params(n_axes, vmem=VMEM_LIMIT):
    return pltpu.CompilerParams(
        dimension_semantics=("arbitrary",) * n_axes, vmem_limit_bytes=vmem)


def _resident(shape, index_map):
    return pl.BlockSpec(shape, index_map, pipeline_mode=pl.Buffered(1))


def _mod_kernel(c_ref, w_ref, b_ref, o_ref):
    cc = c_ref[...]
    s = (cc * jax.nn.sigmoid(cc)).astype(bf16)
    o_ref[0] = _dot(s, w_ref[0].astype(bf16)) + b_ref[0]


def _modulation(c_all, w_mod, b_mod):
    n = N_MOD * D
    return pl.pallas_call(
        _mod_kernel,
        grid=(DEPTH, n // TN_MOD),
        in_specs=[
            pl.BlockSpec((MOD_ROWS, D), lambda i, j: (0, 0)),
            pl.BlockSpec((1, D, TN_MOD), lambda i, j: (i, 0, j)),
            pl.BlockSpec((1, 1, TN_MOD), lambda i, j: (i, 0, j)),
        ],
        out_specs=pl.BlockSpec((1, MOD_ROWS, TN_MOD), lambda i, j: (i, 0, j)),
        out_shape=jax.ShapeDtypeStruct((DEPTH, MOD_ROWS, n), f32),
        compiler_params=_params(2),
        name="modulation",
    )(c_all, w_mod, b_mod.reshape(DEPTH, 1, n))


def _gmlp_kernel(x_ref, mod_ref, ng_ref, win_ref, vg_ref, ws_ref, bs_ref, wout_ref, o_ref):
    x = x_ref[...]
    mod = mod_ref[0, 0]
    ng = ng_ref[0]
    h = _rms(x, ng[0:1]) * (1.0 + mod[1:2]) + mod[0:1]
    z = _dot(h.astype(bf16), win_ref[0])
    z = 0.5 * z * (1.0 + lax.erf(z * np.float32(np.sqrt(0.5))))
    u = z[:, :D]
    v = z[:, D:]
    mu = jnp.mean(v, axis=-1, keepdims=True)
    vc = v - mu
    var = jnp.mean(vc * vc, axis=-1, keepdims=True)
    vn = (vc * lax.rsqrt(var + EPS) * vg_ref[0]).astype(bf16)
    rows = []
    for ck in range(TM_MIX // CHUNK):
        r0 = ck * CHUNK
        cols = []
        for g in range(GROUPS):
            c0 = g * GROUP_DIM
            cols.append(_dot(ws_ref[0, g], vn[r0:r0 + CHUNK, c0:c0 + GROUP_DIM]))
        rows.append(jnp.concatenate(cols, axis=1) + bs_ref[0])
    s = jnp.concatenate(rows, axis=0)
    y = _dot((u * s).astype(bf16), wout_ref[0])
    o_ref[...] = x + mod[2:3] * _rms(y, ng[1:2])


def _gmlp_layer(x_all, mod, norm_g, w_in, v_g, w_s, b_full, w_out, i, j, n_tok):
    tm = TM_MIX
    row = _mod_row(tm)
    return pl.pallas_call(
        _gmlp_kernel,
        grid=(n_tok // tm,),
        in_specs=[
            pl.BlockSpec((tm, D), lambda t: (t, 0)),
            pl.BlockSpec((1, 1, N_MOD, D), lambda t: (i, row(t), 0, 0)),
            pl.BlockSpec((1, 4, D), lambda t: (i, 0, 0)),
            _resident((1, D, 2 * D), lambda t: (j, 0, 0)),
            pl.BlockSpec((1, 1, D), lambda t: (j, 0, 0)),
            _resident((1, GROUPS, CHUNK, CHUNK), lambda t: (j, 0, 0, 0)),
            _resident((1, CHUNK, D), lambda t: (j, 0, 0)),
            _resident((1, D, D), lambda t: (j, 0, 0)),
        ],
        out_specs=pl.BlockSpec((tm, D), lambda t: (t, 0)),
        out_shape=jax.ShapeDtypeStruct((n_tok, D), f32),
        compiler_params=_params(1),
        name="gmlp_mixer",
    )(x_all, mod, norm_g, w_in, v_g, w_s, b_full, w_out)


def _rope(xt, cos, sa, sb):
    return xt * cos + pltpu.roll(xt, 16, 1) * sa + pltpu.roll(xt, LANES - 16, 1) * sb


def _qkv_kernel(x_ref, mod_ref, ng_ref, w_ref, cos_ref, sa_ref, sb_ref, q_ref, k_ref, v_ref):
    x = x_ref[...]
    mod = mod_ref[0, 0]
    h = _rms(x, ng_ref[0][0:1]) * (1.0 + mod[1:2]) + mod[0:1]
    qkv = _dot(h.astype(bf16), w_ref[0])
    cos, sa, sb = cos_ref[...], sa_ref[...], sb_ref[...]
    scale = np.float32(HEAD_DIM ** -0.5)
    for p in range(D // LANES):
        c0 = p * LANES
        q_ref[:, c0:c0 + LANES] = (_rope(qkv[:, c0:c0 + LANES], cos, sa, sb) * scale).astype(bf16)
    lo = lax.broadcasted_iota(jnp.int32, (x.shape[0], LANES), 1) < HEAD_DIM
    for m in range(N_KV_HEADS // 2):
        kt = _rope(qkv[:, D + m * LANES:D + (m + 1) * LANES], cos, sa, sb)
        vt = qkv[:, D + N_KV_HEADS * HEAD_DIM + m * LANES:D + N_KV_HEADS * HEAD_DIM + (m + 1) * LANES]
        for src, dst in ((kt, k_ref), (vt, v_ref)):
            sw = pltpu.roll(src, HEAD_DIM, 1)
            base = 2 * m * 2 * PAIR
            dst[:, base:base + PAIR] = jnp.where(lo, src, 0.0).astype(bf16)
            dst[:, base + PAIR:base + 2 * PAIR] = jnp.where(lo, 0.0, sw).astype(bf16)
            dst[:, base + 2 * PAIR:base + 3 * PAIR] = jnp.where(lo, sw, 0.0).astype(bf16)
            dst[:, base + 3 * PAIR:base + 4 * PAIR] = jnp.where(lo, 0.0, src).astype(bf16)


def _qkv_layer(x_all, mod, norm_g, w_qkv, rope, i, j):
    tm = TM_MIX
    row = _mod_row(tm)
    n_lat_tiles = SEQ // tm
    rope_row = lambda t: (jnp.where(t < T_LAT // tm, t % n_lat_tiles, n_lat_tiles), 0)
    n_qkv = w_qkv.shape[-1]
    return pl.pallas_call(
        _qkv_kernel,
        grid=(T_ALL // tm,),
        in_specs=[
            pl.BlockSpec((tm, D), lambda t: (t, 0)),
            pl.BlockSpec((1, 1, N_MOD, D), lambda t: (i, row(t), 0, 0)),
            pl.BlockSpec((1, 4, D), lambda t: (i, 0, 0)),
            _resident((1, D, n_qkv), lambda t: (j, 0, 0)),
            pl.BlockSpec((tm, LANES), rope_row),
            pl.BlockSpec((tm, LANES), rope_row),
            pl.BlockSpec((tm, LANES), rope_row),
        ],
        out_specs=[
            pl.BlockSpec((tm, D), lambda t: (t, 0)),
            pl.BlockSpec((tm, KV_EXT), lambda t: (t, 0)),
            pl.BlockSpec((tm, KV_EXT), lambda t: (t, 0)),
        ],
        out_shape=[
            jax.ShapeDtypeStruct((T_ALL, D), bf16),
            jax.ShapeDtypeStruct((T_ALL, KV_EXT), bf16),
            jax.ShapeDtypeStruct((T_ALL, KV_EXT), bf16),
        ],
        compiler_params=_params(1),
        name="qkv_rope",
    )(x_all, mod, norm_g, w_qkv, *rope)


def _attn_kernel(sink_ref, q_ref, kc_ref, vc_ref, *rest, layer, with_band):
    if with_band:
        kl_ref, vl_ref, o_ref = rest
        n = pl.program_id(1)
        s0 = pl.multiple_of(jnp.clip((n - 1) * ATT_BLOCK, 0, SEQ - BAND), ATT_BLOCK)
        rel = n * ATT_BLOCK - s0
        m_rows = (KV_GROUP // 2) * ATT_BLOCK
        qi = lax.broadcasted_iota(jnp.int32, (m_rows, BAND), 0) & (ATT_BLOCK - 1)
        kj = lax.broadcasted_iota(jnp.int32, (m_rows, BAND), 1)
        dist = qi - kj + rel
        valid = (dist <= WINDOW) & (dist >= -WINDOW)
    else:
        (o_ref,) = rest
    n_pairs = KV_GROUP // 2
    for kh in range(N_KV_HEADS):
        qp = jnp.concatenate(
            [q_ref[:, (kh * n_pairs + pp) * PAIR:(kh * n_pairs + pp + 1) * PAIR] for pp in range(n_pairs)],
            axis=0)
        acc = None
        for par in range(2):
            c0 = (2 * kh + par) * PAIR
            s_ctx = _dot_nt(qp, kc_ref[:, c0:c0 + PAIR])
            sink = jnp.concatenate(
                [jnp.full((ATT_BLOCK, 1), sink_ref[layer, kh * KV_GROUP + 2 * pp + par], f32)
                 for pp in range(n_pairs)], axis=0)
            m = jnp.maximum(jnp.max(s_ctx, axis=-1, keepdims=True), sink)
            if with_band:
                s_band = _dot_nt(qp, kl_ref[pl.ds(s0, BAND), c0:c0 + PAIR])
                s_band = jnp.where(valid, s_band, NEG_INF)
                m = jnp.maximum(m, jnp.max(s_band, axis=-1, keepdims=True))
            p_ctx = jnp.exp(s_ctx - m)
            den = jnp.sum(p_ctx, axis=-1, keepdims=True) + jnp.exp(sink - m)
            o = _dot(p_ctx.astype(bf16), vc_ref[:, c0:c0 + PAIR])
            if with_band:
                p_band = jnp.exp(s_band - m)
                den = den + jnp.sum(p_band, axis=-1, keepdims=True)
                o = o + _dot(p_band.astype(bf16), vl_ref[pl.ds(s0, BAND), c0:c0 + PAIR])
            o = o / den
            acc = o if acc is None else acc + o
        for pp in range(n_pairs):
            c0 = (kh * n_pairs + pp) * PAIR
            o_ref[:, c0:c0 + PAIR] = acc[pp * ATT_BLOCK:(pp + 1) * ATT_BLOCK].astype(bf16)


def _attention(sink, q, k_ext, v_ext, j, with_band):
    ctx_blk = T_LAT // CTX_LEN
    smem = pl.BlockSpec(memory_space=pltpu.SMEM)
    if with_band:
        nb = SEQ // ATT_BLOCK
        grid = (BATCH, nb)
        q_spec = pl.BlockSpec((ATT_BLOCK, D), lambda b, n: (b * nb + n, 0))
        c_spec = pl.BlockSpec((CTX_LEN, KV_EXT), lambda b, n: (ctx_blk + b, 0))
        l_spec = pl.BlockSpec((SEQ, KV_EXT), lambda b, n: (b, 0))
        in_specs = [smem, q_spec, c_spec, c_spec, l_spec, l_spec]
        args = (sink, q, k_ext, v_ext, k_ext, v_ext)
        n_out = T_LAT
    else:
        nb = CTX_LEN // ATT_BLOCK
        grid = (BATCH, nb)
        q_spec = pl.BlockSpec((ATT_BLOCK, D), lambda b, n: (T_LAT // ATT_BLOCK + b * nb + n, 0))
        c_spec = pl.BlockSpec((CTX_LEN, KV_EXT), lambda b, n: (ctx_blk + b, 0))
        in_specs = [smem, q_spec, c_spec, c_spec]
        args = (sink, q, k_ext, v_ext)
        n_out = T_CTX
    return pl.pallas_call(
        functools.partial(_attn_kernel, layer=j, with_band=with_band),
        grid=grid,
        in_specs=in_specs,
        out_specs=pl.BlockSpec((ATT_BLOCK, D), lambda b, n: (b * nb + n, 0)),
        out_shape=jax.ShapeDtypeStruct((n_out, D), bf16),
        compiler_params=_params(2),
        name="attn_band" if with_band else "attn_ctx",
    )(*args)


def _proj_kernel(x_ref, o_ref_in, mod_ref, ng_ref, w_ref, out_ref):
    y = _dot(o_ref_in[...], w_ref[0])
    out_ref[...] = x_ref[...] + mod_ref[0, 0][2:3] * _rms(y, ng_ref[0][1:2])


def _proj_layer(x_all, o_all, mod, norm_g, w_o, i, j, n_tok):
    tm = TM_MIX
    row = _mod_row(tm)
    return pl.pallas_call(
        _proj_kernel,
        grid=(n_tok // tm,),
        in_specs=[
            pl.BlockSpec((tm, D), lambda t: (t, 0)),
            pl.BlockSpec((tm, D), lambda t: (t, 0)),
            pl.BlockSpec((1, 1, N_MOD, D), lambda t: (i, row(t), 0, 0)),
            pl.BlockSpec((1, 4, D), lambda t: (i, 0, 0)),
            _resident((1, D, D), lambda t: (j, 0, 0)),
        ],
        out_specs=pl.BlockSpec((tm, D), lambda t: (t, 0)),
        out_shape=jax.ShapeDtypeStruct((n_tok, D), f32),
        compiler_params=_params(1),
        name="attn_proj",
    )(x_all, o_all, mod, norm_g, w_o)


def _ffn_kernel(x_ref, mod_ref, ng_ref, wg_ref, wu_ref, wd_ref, o_ref, h_scr, acc_scr):
    hstep = pl.program_id(1)

    @pl.when(hstep == 0)
    def _():
        mod = mod_ref[0, 0]
        h = _rms(x_ref[...], ng_ref[0][2:3]) * (1.0 + mod[4:5]) + mod[3:4]
        h_scr[...] = h.astype(bf16)
        acc_scr[...] = jnp.zeros_like(acc_scr)

    h = h_scr[...]
    g = _dot(h, wg_ref[0])
    u = _dot(h, wu_ref[0])
    a = (g * jax.nn.sigmoid(g) * u).astype(bf16)
    acc_scr[...] += _dot(a, wd_ref[0])

    @pl.when(hstep == pl.num_programs(1) - 1)
    def _():
        o_ref[...] = x_ref[...] + mod_ref[0, 0][5:6] * _rms(acc_scr[...], ng_ref[0][3:4])


def _ffn_layer(x_all, mod, norm_g, w_gu, w_down, i, n_tok):
    tm, th = TM_FFN, TH_FFN
    row = _mod_row(tm)
    nh = FFN_HIDDEN // th
    return pl.pallas_call(
        _ffn_kernel,
        grid=(n_tok // tm, nh),
        in_specs=[
            pl.BlockSpec((tm, D), lambda t, h: (t, 0)),
            pl.BlockSpec((1, 1, N_MOD, D), lambda t, h: (i, row(t), 0, 0)),
            pl.BlockSpec((1, 4, D), lambda t, h: (i, 0, 0)),
            pl.BlockSpec((1, D, th), lambda t, h: (i, 0, h)),
            pl.BlockSpec((1, D, th), lambda t, h: (i, 0, nh + h)),
            pl.BlockSpec((1, th, D), lambda t, h: (i, h, 0)),
        ],
        out_specs=pl.BlockSpec((tm, D), lambda t, h: (t, 0)),
        out_shape=jax.ShapeDtypeStruct((n_tok, D), f32),
        scratch_shapes=[pltpu.VMEM((tm, D), bf16), pltpu.VMEM((tm, D), f32)],
        compiler_params=_params(2),
        name="swiglu_ffn",
    )(x_all, mod, norm_g, w_gu, w_gu, w_down)


def _rope_tables():
    rows = SEQ // GRID_W
    pos = np.stack([np.repeat(np.arange(rows), GRID_W), np.tile(np.arange(GRID_W), rows)], -1)
    n_freq = HEAD_DIM // 4
    inv = jnp.asarray(ROPE_THETA, f32) ** (-jnp.arange(n_freq, dtype=f32) / n_freq)
    ang = jnp.asarray(pos, f32)[:, :, None] * inv
    cos, sin = jnp.cos(ang), jnp.sin(ang)
    zero = jnp.zeros_like(sin)
    cos_h = jnp.concatenate([cos[:, 0], cos[:, 0], cos[:, 1], cos[:, 1]], -1)
    sa_h = jnp.concatenate([zero[:, 0], sin[:, 0], zero[:, 1], sin[:, 1]], -1)
    sb_h = jnp.concatenate([-sin[:, 0], zero[:, 0], -sin[:, 1], zero[:, 1]], -1)
    ident = (jnp.ones((TM_MIX, HEAD_DIM), f32), jnp.zeros((TM_MIX, HEAD_DIM), f32),
             jnp.zeros((TM_MIX, HEAD_DIM), f32))
    return tuple(jnp.tile(jnp.concatenate([t, e], 0), (1, LANES // HEAD_DIM))
                 for t, e in zip((cos_h, sa_h, sb_h), ident))


def kernel(x, c, ctx, c_ctx, w_mod, b_mod, norm_g, a_w_in, a_v_norm_g, a_w_s, a_b_s, a_w_out,
           b_w_qkv, b_sink, b_w_o, f_w_gu, f_w_down):
    assert x.shape == (BATCH, SEQ, D) and ctx.shape == (BATCH, CTX_LEN, D)
    x_all = jnp.concatenate([x.reshape(T_LAT, D), ctx.reshape(T_CTX, D)], axis=0)
    c_all = jnp.concatenate(
        [c, c_ctx[None], jnp.zeros((MOD_ROWS - BATCH - 1, D), f32)], axis=0)
    mod = _modulation(c_all, w_mod, b_mod).reshape(DEPTH, MOD_ROWS, N_MOD, D)

    w_in, w_out = a_w_in.astype(bf16), a_w_out.astype(bf16)
    w_s = a_w_s.astype(bf16)
    b_full = jnp.repeat(jnp.swapaxes(a_b_s, 1, 2), GROUP_DIM, axis=2)
    v_g = a_v_norm_g.reshape(-1, 1, D)
    w_qkv, w_o = b_w_qkv.astype(bf16), b_w_o.astype(bf16)
    w_gu, w_down = f_w_gu.astype(bf16), f_w_down.astype(bf16)
    rope = _rope_tables()

    for i in range(DEPTH):
        last = i == DEPTH - 1
        j = i // 2
        n_tok = T_LAT if last else T_ALL
        if i % 2 == 0:
            x_all = _gmlp_layer(x_all, mod, norm_g, w_in, v_g, w_s, b_full, w_out, i, j, n_tok)
        else:
            q, k_ext, v_ext = _qkv_layer(x_all, mod, norm_g, w_qkv, rope, i, j)
            o = _attention(b_sink, q, k_ext, v_ext, j, True)
            if not last:
                o = jnp.concatenate([o, _attention(b_sink, q, k_ext, v_ext, j, False)], axis=0)
            x_all = _proj_layer(x_all, o, mod, norm_g, w_o, i, j, n_tok)
        x_all = _ffn_layer(x_all, mod, norm_g, w_gu, w_down, i, n_tok)
    return x_all.reshape(BATCH, SEQ, D)
```

```python
import functools

import jax
import jax.numpy as jnp
import numpy as np
from jax import lax
from jax.experimental import pallas as pl
from jax.experimental.pallas import tpu as pltpu

D = 2048
BATCH = 4
SEQ = 2048
DEPTH = 4
GRID_W = 64
CTX_LEN = 256
CHUNK = 128
GROUP_DIM = 128
GROUPS = D // GROUP_DIM
HEAD_DIM = 64
N_Q_HEADS = D // HEAD_DIM
KV_GROUP = 8
N_KV_HEADS = N_Q_HEADS // KV_GROUP
WINDOW = 128
ATT_BLOCK = 128
BAND = 3 * ATT_BLOCK
ROPE_THETA = 10000.0
FFN_HIDDEN = 5632
N_MOD = 6
EPS = 1e-6
NEG_INF = -1e30

T_LAT = BATCH * SEQ
T_CTX = BATCH * CTX_LEN
T_ALL = T_LAT + T_CTX
MOD_ROWS = 8
LANES = 128
PAIR = 2 * HEAD_DIM
KV_EXT = N_KV_HEADS * 2 * PAIR

TM_MIX = 256
TM_FFN = 512
TH_FFN = 512
TN_MOD = 1024
VMEM_LIMIT = 56 * 1024 * 1024

bf16 = jnp.bfloat16
f32 = jnp.float32


def _rms(x, g):
    return x * lax.rsqrt(jnp.mean(x * x, axis=-1, keepdims=True) + EPS) * g


def _dot(a, b):
    return jnp.dot(a, b, preferred_element_type=f32)


def _dot_nt(a, b):
    return lax.dot_general(a, b, (((1,), (1,)), ((), ())), preferred_element_type=f32)


def _mod_row(tm):
    return lambda t: jnp.minimum((t * tm) // SEQ, BATCH)


def _params(n_axes, vmem=VMEM_LIMIT):
    return pltpu.CompilerParams(
        dimension_semantics=("arbitrary",) * n_axes, vmem_limit_bytes=vmem)


def _resident(shape, index_map):
    return pl.BlockSpec(shape, index_map, pipeline_mode=pl.Buffered(1))


def _mod_kernel(c_ref, w_ref, b_ref, o_ref):
    cc = c_ref[...]
    s = (cc * jax.nn.sigmoid(cc)).astype(bf16)
    o_ref[0] = _dot(s, w_ref[0].astype(bf16)) + b_ref[0]


def _modulation(c_all, w_mod, b_mod):
    n = N_MOD * D
    return pl.pallas_call(
        _mod_kernel,
        grid=(DEPTH, n // TN_MOD),
        in_specs=[
            pl.BlockSpec((MOD_ROWS, D), lambda i, j: (0, 0)),
            pl.BlockSpec((1, D, TN_MOD), lambda i, j: (i, 0, j)),
            pl.BlockSpec((1, 1, TN_MOD), lambda i, j: (i, 0, j)),
        ],
        out_specs=pl.BlockSpec((1, MOD_ROWS, TN_MOD), lambda i, j: (i, 0, j)),
        out_shape=jax.ShapeDtypeStruct((DEPTH, MOD_ROWS, n), f32),
        compiler_params=_params(2),
        name="modulation",
    )(c_all, w_mod, b_mod.reshape(DEPTH, 1, n))


def _gmlp_kernel(x_ref, mod_ref, ng_ref, win_ref, vg_ref, ws_ref, bs_ref, wout_ref, o_ref):
    x = x_ref[...]
    mod = mod_ref[0, 0]
    ng = ng_ref[0]
    h = _rms(x, ng[0:1]) * (1.0 + mod[1:2]) + mod[0:1]
    z = _dot(h.astype(bf16), win_ref[0])
    z = 0.5 * z * (1.0 + lax.erf(z * np.float32(np.sqrt(0.5))))
    u = z[:, :D]
    v = z[:, D:]
    mu = jnp.mean(v, axis=-1, keepdims=True)
    vc = v - mu
    var = jnp.mean(vc * vc, axis=-1, keepdims=True)
    vn = (vc * lax.rsqrt(var + EPS) * vg_ref[0]).astype(bf16)
    n_chunks = TM_MIX // CHUNK
    cols = []
    for g in range(GROUPS):
        c0 = g * GROUP_DIM
        vg = jnp.concatenate(
            [vn[ck * CHUNK:(ck + 1) * CHUNK, c0:c0 + GROUP_DIM] for ck in range(n_chunks)], axis=1)
        sg = _dot(ws_ref[0, g], vg)
        cols.append(jnp.concatenate(
            [sg[:, ck * GROUP_DIM:(ck + 1) * GROUP_DIM] for ck in range(n_chunks)], axis=0))
    s = jnp.concatenate(cols, axis=1) + jnp.concatenate([bs_ref[0]] * n_chunks, axis=0)
    y = _dot((u * s).astype(bf16), wout_ref[0])
    o_ref[...] = x + mod[2:3] * _rms(y, ng[1:2])


def _gmlp_layer(x_all, mod, norm_g, w_in, v_g, w_s, b_full, w_out, i, j, n_tok):
    tm = TM_MIX
    row = _mod_row(tm)
    return pl.pallas_call(
        _gmlp_kernel,
        grid=(n_tok // tm,),
        in_specs=[
            pl.BlockSpec((tm, D), lambda t: (t, 0)),
            pl.BlockSpec((1, 1, N_MOD, D), lambda t: (i, row(t), 0, 0)),
            pl.BlockSpec((1, 4, D), lambda t: (i, 0, 0)),
            _resident((1, D, 2 * D), lambda t: (j, 0, 0)),
            pl.BlockSpec((1, 1, D), lambda t: (j, 0, 0)),
            _resident((1, GROUPS, CHUNK, CHUNK), lambda t: (j, 0, 0, 0)),
            _resident((1, CHUNK, D), lambda t: (j, 0, 0)),
            _resident((1, D, D), lambda t: (j, 0, 0)),
        ],
        out_specs=pl.BlockSpec((tm, D), lambda t: (t, 0)),
        out_shape=jax.ShapeDtypeStruct((n_tok, D), f32),
        compiler_params=_params(1),
        name="gmlp_mixer",
    )(x_all, mod, norm_g, w_in, v_g, w_s, b_full, w_out)


def _rope(xt, cos, sa, sb):
    return xt * cos + pltpu.roll(xt, 16, 1) * sa + pltpu.roll(xt, LANES - 16, 1) * sb


def _qkv_kernel(x_ref, mod_ref, ng_ref, w_ref, cos_ref, sa_ref, sb_ref, qt_ref, k_ref, vt_ref):
    x = x_ref[...]
    mod = mod_ref[0, 0]
    h = _rms(x, ng_ref[0][0:1]) * (1.0 + mod[1:2]) + mod[0:1]
    qkv = _dot(h.astype(bf16), w_ref[0])
    cos, sa, sb = cos_ref[...], sa_ref[...], sb_ref[...]
    scale = np.float32(HEAD_DIM ** -0.5)
    n_blk = x.shape[0] // ATT_BLOCK
    for p in range(D // LANES):
        c0 = p * LANES
        qt = _rope(qkv[:, c0:c0 + LANES], cos, sa, sb) * scale
        for r in range(n_blk):
            r0 = r * ATT_BLOCK
            qt_ref[c0:c0 + LANES, r0:r0 + ATT_BLOCK] = qt[r0:r0 + ATT_BLOCK].T.astype(bf16)
    lo = lax.broadcasted_iota(jnp.int32, (x.shape[0], LANES), 1) < HEAD_DIM
    for m in range(N_KV_HEADS // 2):
        kt = _rope(qkv[:, D + m * LANES:D + (m + 1) * LANES], cos, sa, sb)
        vt = qkv[:, D + N_KV_HEADS * HEAD_DIM + m * LANES:D + N_KV_HEADS * HEAD_DIM + (m + 1) * LANES]
        base = 2 * m * 2 * PAIR
        sw = pltpu.roll(kt, HEAD_DIM, 1)
        k_ref[:, base:base + PAIR] = jnp.where(lo, kt, 0.0).astype(bf16)
        k_ref[:, base + PAIR:base + 2 * PAIR] = jnp.where(lo, 0.0, sw).astype(bf16)
        k_ref[:, base + 2 * PAIR:base + 3 * PAIR] = jnp.where(lo, sw, 0.0).astype(bf16)
        k_ref[:, base + 3 * PAIR:base + 4 * PAIR] = jnp.where(lo, 0.0, kt).astype(bf16)
        sw = pltpu.roll(vt, HEAD_DIM, 1)
        parts = (jnp.where(lo, vt, 0.0), jnp.where(lo, 0.0, sw), jnp.where(lo, sw, 0.0), jnp.where(lo, 0.0, vt))
        for e, part in enumerate(parts):
            for r in range(n_blk):
                r0 = r * ATT_BLOCK
                vt_ref[r, base + e * PAIR:base + (e + 1) * PAIR, :] = part[r0:r0 + ATT_BLOCK].T.astype(bf16)


def _qkv_layer(x_all, mod, norm_g, w_qkv, rope, i, j):
    tm = TM_MIX
    row = _mod_row(tm)
    n_lat_tiles = SEQ // tm
    n_blk = tm // ATT_BLOCK
    rope_row = lambda t: (jnp.where(t < T_LAT // tm, t % n_lat_tiles, n_lat_tiles), 0)
    n_qkv = w_qkv.shape[-1]
    return pl.pallas_call(
        _qkv_kernel,
        grid=(T_ALL // tm,),
        in_specs=[
            pl.BlockSpec((tm, D), lambda t: (t, 0)),
            pl.BlockSpec((1, 1, N_MOD, D), lambda t: (i, row(t), 0, 0)),
            pl.BlockSpec((1, 4, D), lambda t: (i, 0, 0)),
            _resident((1, D, n_qkv), lambda t: (j, 0, 0)),
            pl.BlockSpec((tm, LANES), rope_row),
            pl.BlockSpec((tm, LANES), rope_row),
            pl.BlockSpec((tm, LANES), rope_row),
        ],
        out_specs=[
            pl.BlockSpec((D, tm), lambda t: (0, t)),
            pl.BlockSpec((tm, KV_EXT), lambda t: (t, 0)),
            pl.BlockSpec((n_blk, KV_EXT, ATT_BLOCK), lambda t: (t, 0, 0)),
        ],
        out_shape=[
            jax.ShapeDtypeStruct((D, T_ALL), bf16),
            jax.ShapeDtypeStruct((T_ALL, KV_EXT), bf16),
            jax.ShapeDtypeStruct((T_ALL // ATT_BLOCK, KV_EXT, ATT_BLOCK), bf16),
        ],
        compiler_params=_params(1),
        name="qkv_rope",
    )(x_all, mod, norm_g, w_qkv, *rope)


def _attn_kernel(sink_ref, qt_ref, kc_ref, vct_ref, *rest, layer, with_band):
    n_pairs = KV_GROUP // 2
    n_cols = n_pairs * ATT_BLOCK
    if with_band:
        kl_ref, vlt_ref, o_ref = rest
        n = pl.program_id(1)
        blk0 = jnp.clip(n - 1, 0, (SEQ - BAND) // ATT_BLOCK)
        s0 = pl.multiple_of(blk0 * ATT_BLOCK, ATT_BLOCK)
        rel = n * ATT_BLOCK - s0
        kj = lax.broadcasted_iota(jnp.int32, (BAND, n_cols), 0)
        qi = lax.broadcasted_iota(jnp.int32, (BAND, n_cols), 1) & (ATT_BLOCK - 1)
        dist = qi - kj + rel
        valid = (dist <= WINDOW) & (dist >= -WINDOW)
    else:
        (o_ref,) = rest
    for kh in range(N_KV_HEADS):
        qt = jnp.concatenate(
            [qt_ref[(kh * n_pairs + pp) * PAIR:(kh * n_pairs + pp + 1) * PAIR, :] for pp in range(n_pairs)],
            axis=1)
        acc = None
        for par in range(2):
            c0 = (2 * kh + par) * PAIR
            sink = jnp.concatenate(
                [jnp.full((1, ATT_BLOCK), sink_ref[layer, kh * KV_GROUP + 2 * pp + par], f32)
                 for pp in range(n_pairs)], axis=1)
            s_ctx = _dot(kc_ref[:, c0:c0 + PAIR], qt)
            m = jnp.maximum(jnp.max(s_ctx, axis=0, keepdims=True), sink)
            if with_band:
                s_band = _dot(kl_ref[pl.ds(s0, BAND), c0:c0 + PAIR], qt)
                s_band = jnp.where(valid, s_band, NEG_INF)
                m = jnp.maximum(m, jnp.max(s_band, axis=0, keepdims=True))
            p_ctx = jnp.exp(s_ctx - m)
            den = jnp.sum(p_ctx, axis=0, keepdims=True) + jnp.exp(sink - m)
            vct = jnp.concatenate(
                [vct_ref[r, c0:c0 + PAIR, :] for r in range(CTX_LEN // ATT_BLOCK)], axis=1)
            o = _dot(vct, p_ctx.astype(bf16))
            if with_band:
                p_band = jnp.exp(s_band - m)
                den = den + jnp.sum(p_band, axis=0, keepdims=True)
                vlt = jnp.concatenate(
                    [vlt_ref[blk0 + r, c0:c0 + PAIR, :] for r in range(BAND // ATT_BLOCK)], axis=1)
                o = o + _dot(vlt, p_band.astype(bf16))
            o = o / den
            acc = o if acc is None else acc + o
        for pp in range(n_pairs):
            c0 = (kh * n_pairs + pp) * PAIR
            o_ref[:, c0:c0 + PAIR] = acc[:, pp * ATT_BLOCK:(pp + 1) * ATT_BLOCK].T.astype(bf16)


def _attention(sink, qt, k_ext, vt_ext, j, with_band):
    ctx_blk = T_LAT // CTX_LEN
    ctx_sub = CTX_LEN // ATT_BLOCK
    smem = pl.BlockSpec(memory_space=pltpu.SMEM)
    kc_spec = pl.BlockSpec((CTX_LEN, KV_EXT), lambda b, n: (ctx_blk + b, 0))
    vc_spec = pl.BlockSpec((ctx_sub, KV_EXT, ATT_BLOCK), lambda b, n: (ctx_blk + b, 0, 0))
    if with_band:
        nb = SEQ // ATT_BLOCK
        q_spec = pl.BlockSpec((D, ATT_BLOCK), lambda b, n: (0, b * nb + n))
        kl_spec = pl.BlockSpec((SEQ, KV_EXT), lambda b, n: (b, 0))
        vl_spec = pl.BlockSpec((nb, KV_EXT, ATT_BLOCK), lambda b, n: (b, 0, 0))
        in_specs = [smem, q_spec, kc_spec, vc_spec, kl_spec, vl_spec]
        args = (sink, qt, k_ext, vt_ext, k_ext, vt_ext)
        n_out = T_LAT
    else:
        nb = ctx_sub
        q_spec = pl.BlockSpec((D, ATT_BLOCK), lambda b, n: (0, T_LAT // ATT_BLOCK + b * nb + n))
        in_specs = [smem, q_spec, kc_spec, vc_spec]
        args = (sink, qt, k_ext, vt_ext)
        n_out = T_CTX
    grid = (BATCH, nb)
    return pl.pallas_call(
        functools.partial(_attn_kernel, layer=j, with_band=with_band),
        grid=grid,
        in_specs=in_specs,
        out_specs=pl.BlockSpec((ATT_BLOCK, D), lambda b, n: (b * nb + n, 0)),
        out_shape=jax.ShapeDtypeStruct((n_out, D), bf16),
        compiler_params=_params(2),
        name="attn_band" if with_band else "attn_ctx",
    )(*args)


def _proj_kernel(x_ref, o_ref_in, mod_ref, ng_ref, w_ref, out_ref):
    y = _dot(o_ref_in[...], w_ref[0])
    out_ref[...] = x_ref[...] + mod_ref[0, 0][2:3] * _rms(y, ng_ref[0][1:2])


def _proj_layer(x_all, o_all, mod, norm_g, w_o, i, j, n_tok):
    tm = TM_MIX
    row = _mod_row(tm)
    return pl.pallas_call(
        _proj_kernel,
        grid=(n_tok // tm,),
        in_specs=[
            pl.BlockSpec((tm, D), lambda t: (t, 0)),
            pl.BlockSpec((tm, D), lambda t: (t, 0)),
            pl.BlockSpec((1, 1, N_MOD, D), lambda t: (i, row(t), 0, 0)),
            pl.BlockSpec((1, 4, D), lambda t: (i, 0, 0)),
            _resident((1, D, D), lambda t: (j, 0, 0)),
        ],
        out_specs=pl.BlockSpec((tm, D), lambda t: (t, 0)),
        out_shape=jax.ShapeDtypeStruct((n_tok, D), f32),
        compiler_params=_params(1),
        name="attn_proj",
    )(x_all, o_all, mod, norm_g, w_o)


def _ffn_kernel(x_ref, mod_ref, ng_ref, wg_ref, wu_ref, wd_ref, o_ref, h_scr, acc_scr):
    hstep = pl.program_id(1)

    @pl.when(hstep == 0)
    def _():
        mod = mod_ref[0, 0]
        h = _rms(x_ref[...], ng_ref[0][2:3]) * (1.0 + mod[4:5]) + mod[3:4]
        h_scr[...] = h.astype(bf16)
        acc_scr[...] = jnp.zeros_like(acc_scr)

    h = h_scr[...]
    g = _dot(h, wg_ref[0])
    u = _dot(h, wu_ref[0])
    a = (g * jax.nn.sigmoid(g) * u).astype(bf16)
    acc_scr[...] += _dot(a, wd_ref[0])

    @pl.when(hstep == pl.num_programs(1) - 1)
    def _():
        o_ref[...] = x_ref[...] + mod_ref[0, 0][5:6] * _rms(acc_scr[...], ng_ref[0][3:4])


def _ffn_layer(x_all, mod, norm_g, w_gu, w_down, i, n_tok):
    tm, th = TM_FFN, TH_FFN
    row = _mod_row(tm)
    nh = FFN_HIDDEN // th
    return pl.pallas_call(
        _ffn_kernel,
        grid=(n_tok // tm, nh),
        in_specs=[
            pl.BlockSpec((tm, D), lambda t, h: (t, 0)),
            pl.BlockSpec((1, 1, N_MOD, D), lambda t, h: (i, row(t), 0, 0)),
            pl.BlockSpec((1, 4, D), lambda t, h: (i, 0, 0)),
            pl.BlockSpec((1, D, th), lambda t, h: (i, 0, h)),
            pl.BlockSpec((1, D, th), lambda t, h: (i, 0, nh + h)),
            pl.BlockSpec((1, th, D), lambda t, h: (i, h, 0)),
        ],
        out_specs=pl.BlockSpec((tm, D), lambda t, h: (t, 0)),
        out_shape=jax.ShapeDtypeStruct((n_tok, D), f32),
        scratch_shapes=[pltpu.VMEM((tm, D), bf16), pltpu.VMEM((tm, D), f32)],
        compiler_params=_params(2),
        name="swiglu_ffn",
    )(x_all, mod, norm_g, w_gu, w_gu, w_down)


def _rope_tables():
    rows = SEQ // GRID_W
    pos = np.stack([np.repeat(np.arange(rows), GRID_W), np.tile(np.arange(GRID_W), rows)], -1)
    n_freq = HEAD_DIM // 4
    inv = jnp.asarray(ROPE_THETA, f32) ** (-jnp.arange(n_freq, dtype=f32) / n_freq)
    ang = jnp.asarray(pos, f32)[:, :, None] * inv
    cos, sin = jnp.cos(ang), jnp.sin(ang)
    zero = jnp.zeros_like(sin)
    cos_h = jnp.concatenate([cos[:, 0], cos[:, 0], cos[:, 1], cos[:, 1]], -1)
    sa_h = jnp.concatenate([zero[:, 0], sin[:, 0], zero[:, 1], sin[:, 1]], -1)
    sb_h = jnp.concatenate([-sin[:, 0], zero[:, 0], -sin[:, 1], zero[:, 1]], -1)
    ident = (jnp.ones((TM_MIX, HEAD_DIM), f32), jnp.zeros((TM_MIX, HEAD_DIM), f32),
             jnp.zeros((TM_MIX, HEAD_DIM), f32))
    return tuple(jnp.tile(jnp.concatenate([t, e], 0), (1, LANES // HEAD_DIM))
                 for t, e in zip((cos_h, sa_h, sb_h), ident))


def kernel(x, c, ctx, c_ctx, w_mod, b_mod, norm_g, a_w_in, a_v_norm_g, a_w_s, a_b_s, a_w_out,
           b_w_qkv, b_sink, b_w_o, f_w_gu, f_w_down):
    assert x.shape == (BATCH, SEQ, D) and ctx.shape == (BATCH, CTX_LEN, D)
    x_all = jnp.concatenate([x.reshape(T_LAT, D), ctx.reshape(T_CTX, D)], axis=0)
    c_all = jnp.concatenate(
        [c, c_ctx[None], jnp.zeros((MOD_ROWS - BATCH - 1, D), f32)], axis=0)
    mod = _modulation(c_all, w_mod, b_mod).reshape(DEPTH, MOD_ROWS, N_MOD, D)

    w_in, w_out = a_w_in.astype(bf16), a_w_out.astype(bf16)
    w_s = a_w_s.astype(bf16)
    b_full = jnp.repeat(jnp.swapaxes(a_b_s, 1, 2), GROUP_DIM, axis=2)
    v_g = a_v_norm_g.reshape(-1, 1, D)
    w_qkv, w_o = b_w_qkv.astype(bf16), b_w_o.astype(bf16)
    w_gu, w_down = f_w_gu.astype(bf16), f_w_down.astype(bf16)
    rope = _rope_tables()

    for i in range(DEPTH):
        last = i == DEPTH - 1
        j = i // 2
        n_tok = T_LAT if last else T_ALL
        if i % 2 == 0:
            x_all = _gmlp_layer(x_all, mod, norm_g, w_in, v_g, w_s, b_full, w_out, i, j, n_tok)
        else:
            q, k_ext, v_ext = _qkv_layer(x_all, mod, norm_g, w_qkv, rope, i, j)
            o = _attention(b_sink, q, k_ext, v_ext, j, True)
            if not last:
                o = jnp.concatenate([o, _attention(b_sink, q, k_ext, v_ext, j, False)], axis=0)
            x_all = _proj_layer(x_all, o, mod, norm_g, w_o, i, j, n_tok)
        x_all = _ffn_layer(x_all, mod, norm_g, w_gu, w_down, i, n_tok)
    return x_all.reshape(BATCH, SEQ, D)
```

```python
import functools

import jax
import jax.numpy as jnp
import numpy as np
from jax import lax
from jax.experimental import pallas as pl
from jax.experimental.pallas import tpu as pltpu

D = 2048
BATCH = 4
SEQ = 2048
DEPTH = 4
GRID_W = 64
CTX_LEN = 256
CHUNK = 128
GROUP_DIM = 128
GROUPS = D // GROUP_DIM
HEAD_DIM = 64
N_Q_HEADS = D // HEAD_DIM
KV_GROUP = 8
N_KV_HEADS = N_Q_HEADS // KV_GROUP
WINDOW = 128
ATT_BLOCK = 128
BAND = 3 * ATT_BLOCK
ROPE_THETA = 10000.0
FFN_HIDDEN = 5632
N_MOD = 6
EPS = 1e-6
NEG_INF = -1e30

T_LAT = BATCH * SEQ
T_CTX = BATCH * CTX_LEN
T_ALL = T_LAT + T_CTX
MOD_ROWS = 8
LANES = 128
PAIR = 2 * HEAD_DIM
KV_EXT = N_KV_HEADS * 2 * PAIR

TM_MIX = 256
TM_FFN = 1024
TH_FFN = 256
FFN_DOWN_COLS = 512
ROW_CHUNK = 16
ROW_UNROLL = 4
TN_MOD = 1024
VMEM_LIMIT = 56 * 1024 * 1024
FFN_VMEM_LIMIT = 60 * 1024 * 1024

bf16 = jnp.bfloat16
f32 = jnp.float32


def _rms(x, g):
    return x * lax.rsqrt(jnp.mean(x * x, axis=-1, keepdims=True) + EPS) * g


def _dot(a, b):
    return jnp.dot(a, b, preferred_element_type=f32)


def _dot_nt(a, b):
    return lax.dot_general(a, b, (((1,), (1,)), ((), ())), preferred_element_type=f32)


def _mod_row(tm):
    return lambda t: jnp.minimum((t * tm) // SEQ, BATCH)


def _params(n_axes, vmem=VMEM_LIMIT):
    return pltpu.CompilerParams(
        dimension_semantics=("arbitrary",) * n_axes, vmem_limit_bytes=vmem)


def _resident(shape, index_map):
    return pl.BlockSpec(shape, index_map, pipeline_mode=pl.Buffered(1))


def _mod_kernel(c_ref, w_ref, b_ref, o_ref):
    cc = c_ref[...]
    s = (cc * jax.nn.sigmoid(cc)).astype(bf16)
    o_ref[0] = _dot(s, w_ref[0].astype(bf16)) + b_ref[0]


def _modulation(c_all, w_mod, b_mod):
    n = N_MOD * D
    return pl.pallas_call(
        _mod_kernel,
        grid=(DEPTH, n // TN_MOD),
        in_specs=[
            pl.BlockSpec((MOD_ROWS, D), lambda i, j: (0, 0)),
            pl.BlockSpec((1, D, TN_MOD), lambda i, j: (i, 0, j)),
            pl.BlockSpec((1, 1, TN_MOD), lambda i, j: (i, 0, j)),
        ],
        out_specs=pl.BlockSpec((1, MOD_ROWS, TN_MOD), lambda i, j: (i, 0, j)),
        out_shape=jax.ShapeDtypeStruct((DEPTH, MOD_ROWS, n), f32),
        compiler_params=_params(2),
        name="modulation",
    )(c_all, w_mod, b_mod.reshape(DEPTH, 1, n))


def _gmlp_kernel(x_ref, mod_ref, ng_ref, win_ref, vg_ref, ws_ref, bs_ref, wout_ref, o_ref):
    x = x_ref[...]
    mod = mod_ref[0, 0]
    ng = ng_ref[0]
    h = _rms(x, ng[0:1]) * (1.0 + mod[1:2]) + mod[0:1]
    z = _dot(h.astype(bf16), win_ref[0])
    z = 0.5 * z * (1.0 + lax.erf(z * np.float32(np.sqrt(0.5))))
    u = z[:, :D]
    v = z[:, D:]
    mu = jnp.mean(v, axis=-1, keepdims=True)
    vc = v - mu
    var = jnp.mean(vc * vc, axis=-1, keepdims=True)
    vn = (vc * lax.rsqrt(var + EPS) * vg_ref[0]).astype(bf16)
    n_chunks = TM_MIX // CHUNK
    cols = []
    for g in range(GROUPS):
        c0 = g * GROUP_DIM
        vg = jnp.concatenate(
            [vn[ck * CHUNK:(ck + 1) * CHUNK, c0:c0 + GROUP_DIM] for ck in range(n_chunks)], axis=1)
        sg = _dot(ws_ref[0, g], vg)
        cols.append(jnp.concatenate(
            [sg[:, ck * GROUP_DIM:(ck + 1) * GROUP_DIM] for ck in range(n_chunks)], axis=0))
    s = jnp.concatenate(cols, axis=1) + jnp.concatenate([bs_ref[0]] * n_chunks, axis=0)
    y = _dot((u * s).astype(bf16), wout_ref[0])
    o_ref[...] = x + mod[2:3] * _rms(y, ng[1:2])


def _gmlp_layer(x_all, mod, norm_g, w_in, v_g, w_s, b_full, w_out, i, j, n_tok):
    tm = TM_MIX
    row = _mod_row(tm)
    return pl.pallas_call(
        _gmlp_kernel,
        grid=(n_tok // tm,),
        in_specs=[
            pl.BlockSpec((tm, D), lambda t: (t, 0)),
            pl.BlockSpec((1, 1, N_MOD, D), lambda t: (i, row(t), 0, 0)),
            pl.BlockSpec((1, 4, D), lambda t: (i, 0, 0)),
            _resident((1, D, 2 * D), lambda t: (j, 0, 0)),
            pl.BlockSpec((1, 1, D), lambda t: (j, 0, 0)),
            _resident((1, GROUPS, CHUNK, CHUNK), lambda t: (j, 0, 0, 0)),
            _resident((1, CHUNK, D), lambda t: (j, 0, 0)),
            _resident((1, D, D), lambda t: (j, 0, 0)),
        ],
        out_specs=pl.BlockSpec((tm, D), lambda t: (t, 0)),
        out_shape=jax.ShapeDtypeStruct((n_tok, D), f32),
        compiler_params=_params(1),
        name="gmlp_mixer",
    )(x_all, mod, norm_g, w_in, v_g, w_s, b_full, w_out)


def _rope(xt, cos, sa, sb):
    return xt * cos + pltpu.roll(xt, 16, 1) * sa + pltpu.roll(xt, LANES - 16, 1) * sb


def _qkv_kernel(x_ref, mod_ref, ng_ref, w_ref, cos_ref, sa_ref, sb_ref, qt_ref, k_ref, vt_ref):
    x = x_ref[...]
    mod = mod_ref[0, 0]
    h = _rms(x, ng_ref[0][0:1]) * (1.0 + mod[1:2]) + mod[0:1]
    qkv = _dot(h.astype(bf16), w_ref[0])
    cos, sa, sb = cos_ref[...], sa_ref[...], sb_ref[...]
    scale = np.float32(HEAD_DIM ** -0.5)
    n_blk = x.shape[0] // ATT_BLOCK
    for p in range(D // LANES):
        c0 = p * LANES
        qt = _rope(qkv[:, c0:c0 + LANES], cos, sa, sb) * scale
        for r in range(n_blk):
            r0 = r * ATT_BLOCK
            qt_ref[c0:c0 + LANES, r0:r0 + ATT_BLOCK] = qt[r0:r0 + ATT_BLOCK].T.astype(bf16)
    lo = lax.broadcasted_iota(jnp.int32, (x.shape[0], LANES), 1) < HEAD_DIM
    for m in range(N_KV_HEADS // 2):
        kt = _rope(qkv[:, D + m * LANES:D + (m + 1) * LANES], cos, sa, sb)
        vt = qkv[:, D + N_KV_HEADS * HEAD_DIM + m * LANES:D + N_KV_HEADS * HEAD_DIM + (m + 1) * LANES]
        base = 2 * m * 2 * PAIR
        sw = pltpu.roll(kt, HEAD_DIM, 1)
        k_ref[:, base:base + PAIR] = jnp.where(lo, kt, 0.0).astype(bf16)
        k_ref[:, base + PAIR:base + 2 * PAIR] = jnp.where(lo, 0.0, sw).astype(bf16)
        k_ref[:, base + 2 * PAIR:base + 3 * PAIR] = jnp.where(lo, sw, 0.0).astype(bf16)
        k_ref[:, base + 3 * PAIR:base + 4 * PAIR] = jnp.where(lo, 0.0, kt).astype(bf16)
        sw = pltpu.roll(vt, HEAD_DIM, 1)
        parts = (jnp.where(lo, vt, 0.0), jnp.where(lo, 0.0, sw), jnp.where(lo, sw, 0.0), jnp.where(lo, 0.0, vt))
        for e, part in enumerate(parts):
            for r in range(n_blk):
                r0 = r * ATT_BLOCK
                vt_ref[r, base + e * PAIR:base + (e + 1) * PAIR, :] = part[r0:r0 + ATT_BLOCK].T.astype(bf16)


def _qkv_layer(x_all, mod, norm_g, w_qkv, rope, i, j):
    tm = TM_MIX
    row = _mod_row(tm)
    n_lat_tiles = SEQ // tm
    n_blk = tm // ATT_BLOCK
    rope_row = lambda t: (jnp.where(t < T_LAT // tm, t % n_lat_tiles, n_lat_tiles), 0)
    n_qkv = w_qkv.shape[-1]
    return pl.pallas_call(
        _qkv_kernel,
        grid=(T_ALL // tm,),
        in_specs=[
            pl.BlockSpec((tm, D), lambda t: (t, 0)),
            pl.BlockSpec((1, 1, N_MOD, D), lambda t: (i, row(t), 0, 0)),
            pl.BlockSpec((1, 4, D), lambda t: (i, 0, 0)),
            _resident((1, D, n_qkv), lambda t: (j, 0, 0)),
            pl.BlockSpec((tm, LANES), rope_row),
            pl.BlockSpec((tm, LANES), rope_row),
            pl.BlockSpec((tm, LANES), rope_row),
        ],
        out_specs=[
            pl.BlockSpec((D, tm), lambda t: (0, t)),
            pl.BlockSpec((tm, KV_EXT), lambda t: (t, 0)),
            pl.BlockSpec((n_blk, KV_EXT, ATT_BLOCK), lambda t: (t, 0, 0)),
        ],
        out_shape=[
            jax.ShapeDtypeStruct((D, T_ALL), bf16),
            jax.ShapeDtypeStruct((T_ALL, KV_EXT), bf16),
            jax.ShapeDtypeStruct((T_ALL // ATT_BLOCK, KV_EXT, ATT_BLOCK), bf16),
        ],
        compiler_params=_params(1),
        name="qkv_rope",
    )(x_all, mod, norm_g, w_qkv, *rope)


def _attn_kernel(sink_ref, qt_ref, kc_ref, vct_ref, *rest, layer, with_band):
    n_pairs = KV_GROUP // 2
    n_cols = n_pairs * ATT_BLOCK
    if with_band:
        kl_ref, vlt_ref, o_ref = rest
        n = pl.program_id(1)
        blk0 = jnp.clip(n - 1, 0, (SEQ - BAND) // ATT_BLOCK)
        s0 = pl.multiple_of(blk0 * ATT_BLOCK, ATT_BLOCK)
        rel = n * ATT_BLOCK - s0
        kj = lax.broadcasted_iota(jnp.int32, (BAND, n_cols), 0)
        qi = lax.broadcasted_iota(jnp.int32, (BAND, n_cols), 1) & (ATT_BLOCK - 1)
        dist = qi - kj + rel
        valid = (dist <= WINDOW) & (dist >= -WINDOW)
    else:
        (o_ref,) = rest
    for kh in range(N_KV_HEADS):
        qt = jnp.concatenate(
            [qt_ref[(kh * n_pairs + pp) * PAIR:(kh * n_pairs + pp + 1) * PAIR, :] for pp in range(n_pairs)],
            axis=1)
        acc = None
        for par in range(2):
            c0 = (2 * kh + par) * PAIR
            sink = jnp.concatenate(
                [jnp.full((1, ATT_BLOCK), sink_ref[layer, kh * KV_GROUP + 2 * pp + par], f32)
                 for pp in range(n_pairs)], axis=1)
            s_ctx = _dot(kc_ref[:, c0:c0 + PAIR], qt)
            m = jnp.maximum(jnp.max(s_ctx, axis=0, keepdims=True), sink)
            if with_band:
                s_band = _dot(kl_ref[pl.ds(s0, BAND), c0:c0 + PAIR], qt)
                s_band = jnp.where(valid, s_band, NEG_INF)
                m = jnp.maximum(m, jnp.max(s_band, axis=0, keepdims=True))
            p_ctx = jnp.exp(s_ctx - m)
            den = jnp.sum(p_ctx, axis=0, keepdims=True) + jnp.exp(sink - m)
            vct = jnp.concatenate(
                [vct_ref[r, c0:c0 + PAIR, :] for r in range(CTX_LEN // ATT_BLOCK)], axis=1)
            o = _dot(vct, p_ctx.astype(bf16))
            if with_band:
                p_band = jnp.exp(s_band - m)
                den = den + jnp.sum(p_band, axis=0, keepdims=True)
                vlt = jnp.concatenate(
                    [vlt_ref[blk0 + r, c0:c0 + PAIR, :] for r in range(BAND // ATT_BLOCK)], axis=1)
                o = o + _dot(vlt, p_band.astype(bf16))
            o = o / den
            acc = o if acc is None else acc + o
        for pp in range(n_pairs):
            c0 = (kh * n_pairs + pp) * PAIR
            o_ref[:, c0:c0 + PAIR] = acc[:, pp * ATT_BLOCK:(pp + 1) * ATT_BLOCK].T.astype(bf16)


def _attention(sink, qt, k_ext, vt_ext, j, with_band):
    ctx_blk = T_LAT // CTX_LEN
    ctx_sub = CTX_LEN // ATT_BLOCK
    smem = pl.BlockSpec(memory_space=pltpu.SMEM)
    kc_spec = pl.BlockSpec((CTX_LEN, KV_EXT), lambda b, n: (ctx_blk + b, 0))
    vc_spec = pl.BlockSpec((ctx_sub, KV_EXT, ATT_BLOCK), lambda b, n: (ctx_blk + b, 0, 0))
    if with_band:
        nb = SEQ // ATT_BLOCK
        q_spec = pl.BlockSpec((D, ATT_BLOCK), lambda b, n: (0, b * nb + n))
        kl_spec = pl.BlockSpec((SEQ, KV_EXT), lambda b, n: (b, 0))
        vl_spec = pl.BlockSpec((nb, KV_EXT, ATT_BLOCK), lambda b, n: (b, 0, 0))
        in_specs = [smem, q_spec, kc_spec, vc_spec, kl_spec, vl_spec]
        args = (sink, qt, k_ext, vt_ext, k_ext, vt_ext)
        n_out = T_LAT
    else:
        nb = ctx_sub
        q_spec = pl.BlockSpec((D, ATT_BLOCK), lambda b, n: (0, T_LAT // ATT_BLOCK + b * nb + n))
        in_specs = [smem, q_spec, kc_spec, vc_spec]
        args = (sink, qt, k_ext, vt_ext)
        n_out = T_CTX
    grid = (BATCH, nb)
    return pl.pallas_call(
        functools.partial(_attn_kernel, layer=j, with_band=with_band),
        grid=grid,
        in_specs=in_specs,
        out_specs=pl.BlockSpec((ATT_BLOCK, D), lambda b, n: (b * nb + n, 0)),
        out_shape=jax.ShapeDtypeStruct((n_out, D), bf16),
        compiler_params=_params(2),
        name="attn_band" if with_band else "attn_ctx",
    )(*args)


def _proj_kernel(x_ref, o_ref_in, mod_ref, ng_ref, w_ref, out_ref):
    y = _dot(o_ref_in[...], w_ref[0])
    out_ref[...] = x_ref[...] + mod_ref[0, 0][2:3] * _rms(y, ng_ref[0][1:2])


def _proj_layer(x_all, o_all, mod, norm_g, w_o, i, j, n_tok):
    tm = TM_MIX
    row = _mod_row(tm)
    return pl.pallas_call(
        _proj_kernel,
        grid=(n_tok // tm,),
        in_specs=[
            pl.BlockSpec((tm, D), lambda t: (t, 0)),
            pl.BlockSpec((tm, D), lambda t: (t, 0)),
            pl.BlockSpec((1, 1, N_MOD, D), lambda t: (i, row(t), 0, 0)),
            pl.BlockSpec((1, 4, D), lambda t: (i, 0, 0)),
            _resident((1, D, D), lambda t: (j, 0, 0)),
        ],
        out_specs=pl.BlockSpec((tm, D), lambda t: (t, 0)),
        out_shape=jax.ShapeDtypeStruct((n_tok, D), f32),
        compiler_params=_params(1),
        name="attn_proj",
    )(x_all, o_all, mod, norm_g, w_o)


def _row_loop(n_rows, body):
    def step(r, carry):
        body(pl.ds(pl.multiple_of(r * ROW_CHUNK, ROW_CHUNK), ROW_CHUNK))
        return carry
    lax.fori_loop(0, n_rows // ROW_CHUNK, step, 0, unroll=ROW_UNROLL)


def _ffn_kernel(x_ref, mod_ref, ng_ref, wg_ref, wu_ref, wd_ref, o_ref, h_scr, acc_scr):
    hstep = pl.program_id(1)
    tm = x_ref.shape[0]

    @pl.when(hstep == 0)
    def _():
        mod = mod_ref[0, 0]
        gain = ng_ref[0][2:3] * (1.0 + mod[4:5])
        shift = mod[3:4]

        def body(rows):
            xr = x_ref[rows, :]
            inv = lax.rsqrt(jnp.mean(xr * xr, axis=-1, keepdims=True) + EPS)
            h_scr[rows, :] = (xr * inv * gain + shift).astype(bf16)
            acc_scr[rows, :] = jnp.zeros((ROW_CHUNK, D), f32)
        _row_loop(tm, body)

    h = h_scr[...]
    g = _dot(h, wg_ref[0].astype(bf16))
    u = _dot(h, wu_ref[0].astype(bf16))
    a = (g * jax.nn.sigmoid(g) * u).astype(bf16)
    wd = wd_ref[0].astype(bf16)
    for c in range(D // FFN_DOWN_COLS):
        cols = slice(c * FFN_DOWN_COLS, (c + 1) * FFN_DOWN_COLS)
        acc_scr[:, cols] += _dot(a, wd[:, cols])

    @pl.when(hstep == pl.num_programs(1) - 1)
    def _():
        gain = mod_ref[0, 0][5:6] * ng_ref[0][3:4]

        def body(rows):
            y = acc_scr[rows, :]
            inv = lax.rsqrt(jnp.mean(y * y, axis=-1, keepdims=True) + EPS)
            o_ref[rows, :] = x_ref[rows, :] + y * inv * gain
        _row_loop(tm, body)


def _ffn_layer(x_all, mod, norm_g, w_gu, w_down, i, n_tok):
    tm, th = TM_FFN, TH_FFN
    row = _mod_row(tm)
    nh = FFN_HIDDEN // th
    return pl.pallas_call(
        _ffn_kernel,
        grid=(n_tok // tm, nh),
        in_specs=[
            pl.BlockSpec((tm, D), lambda t, h: (t, 0)),
            pl.BlockSpec((1, 1, N_MOD, D), lambda t, h: (i, row(t), 0, 0)),
            pl.BlockSpec((1, 4, D), lambda t, h: (i, 0, 0)),
            pl.BlockSpec((1, D, th), lambda t, h: (i, 0, h)),
            pl.BlockSpec((1, D, th), lambda t, h: (i, 0, nh + h)),
            pl.BlockSpec((1, th, D), lambda t, h: (i, h, 0)),
        ],
        out_specs=pl.BlockSpec((tm, D), lambda t, h: (t, 0)),
        out_shape=jax.ShapeDtypeStruct((n_tok, D), f32),
        scratch_shapes=[pltpu.VMEM((tm, D), bf16), pltpu.VMEM((tm, D), f32)],
        compiler_params=_params(2, FFN_VMEM_LIMIT),
        name="swiglu_ffn",
    )(x_all, mod, norm_g, w_gu, w_gu, w_down)


def _rope_tables():
    rows = SEQ // GRID_W
    pos = np.stack([np.repeat(np.arange(rows), GRID_W), np.tile(np.arange(GRID_W), rows)], -1)
    n_freq = HEAD_DIM // 4
    inv = jnp.asarray(ROPE_THETA, f32) ** (-jnp.arange(n_freq, dtype=f32) / n_freq)
    ang = jnp.asarray(pos, f32)[:, :, None] * inv
    cos, sin = jnp.cos(ang), jnp.sin(ang)
    zero = jnp.zeros_like(sin)
    cos_h = jnp.concatenate([cos[:, 0], cos[:, 0], cos[:, 1], cos[:, 1]], -1)
    sa_h = jnp.concatenate([zero[:, 0], sin[:, 0], zero[:, 1], sin[:, 1]], -1)
    sb_h = jnp.concatenate([-sin[:, 0], zero[:, 0], -sin[:, 1], zero[:, 1]], -1)
    ident = (jnp.ones((TM_MIX, HEAD_DIM), f32), jnp.zeros((TM_MIX, HEAD_DIM), f32),
             jnp.zeros((TM_MIX, HEAD_DIM), f32))
    return tuple(jnp.tile(jnp.concatenate([t, e], 0), (1, LANES // HEAD_DIM))
                 for t, e in zip((cos_h, sa_h, sb_h), ident))


def kernel(x, c, ctx, c_ctx, w_mod, b_mod, norm_g, a_w_in, a_v_norm_g, a_w_s, a_b_s, a_w_out,
           b_w_qkv, b_sink, b_w_o, f_w_gu, f_w_down):
    assert x.shape == (BATCH, SEQ, D) and ctx.shape == (BATCH, CTX_LEN, D)
    x_all = jnp.concatenate([x.reshape(T_LAT, D), ctx.reshape(T_CTX, D)], axis=0)
    c_all = jnp.concatenate(
        [c, c_ctx[None], jnp.zeros((MOD_ROWS - BATCH - 1, D), f32)], axis=0)
    mod = _modulation(c_all, w_mod, b_mod).reshape(DEPTH, MOD_ROWS, N_MOD, D)

    w_in, w_out = a_w_in.astype(bf16), a_w_out.astype(bf16)
    w_s = a_w_s.astype(bf16)
    b_full = jnp.repeat(jnp.swapaxes(a_b_s, 1, 2), GROUP_DIM, axis=2)
    v_g = a_v_norm_g.reshape(-1, 1, D)
    w_qkv, w_o = b_w_qkv.astype(bf16), b_w_o.astype(bf16)
    w_gu, w_down = f_w_gu, f_w_down
    rope = _rope_tables()

    for i in range(DEPTH):
        last = i == DEPTH - 1
        j = i // 2
        n_tok = T_LAT if last else T_ALL
        if i % 2 == 0:
            x_all = _gmlp_layer(x_all, mod, norm_g, w_in, v_g, w_s, b_full, w_out, i, j, n_tok)
        else:
            q, k_ext, v_ext = _qkv_layer(x_all, mod, norm_g, w_qkv, rope, i, j)
            o = _attention(b_sink, q, k_ext, v_ext, j, True)
            if not last:
                o = jnp.concatenate([o, _attention(b_sink, q, k_ext, v_ext, j, False)], axis=0)
            x_all = _proj_layer(x_all, o, mod, norm_g, w_o, i, j, n_tok)
        x_all = _ffn_layer(x_all, mod, norm_g, w_gu, w_down, i, n_tok)
    return x_all.reshape(BATCH, SEQ, D)
```

```python
import functools

import jax
import jax.numpy as jnp
import numpy as np
from jax import lax
from jax.experimental import pallas as pl
from jax.experimental.pallas import tpu as pltpu

D = 2048
BATCH = 4
SEQ = 2048
DEPTH = 4
GRID_W = 64
CTX_LEN = 256
CHUNK = 128
GROUP_DIM = 128
GROUPS = D // GROUP_DIM
HEAD_DIM = 64
N_Q_HEADS = D // HEAD_DIM
KV_GROUP = 8
N_KV_HEADS = N_Q_HEADS // KV_GROUP
WINDOW = 128
ATT_BLOCK = 128
BAND = 3 * ATT_BLOCK
ROPE_THETA = 10000.0
FFN_HIDDEN = 5632
N_MOD = 6
EPS = 1e-6
NEG_INF = -1e30

T_LAT = BATCH * SEQ
T_CTX = BATCH * CTX_LEN
T_ALL = T_LAT + T_CTX
MOD_ROWS = 8
LANES = 128
LOG2E = float(np.log2(np.e))
PAIR = 2 * HEAD_DIM
KV_EXT = N_KV_HEADS * 2 * PAIR

TM_MIX = 256
TM_FFN = 1024
TH_FFN = 256
FFN_DOWN_COLS = 512
ROW_CHUNK = 16
ROW_UNROLL = 4
TN_MOD = 1024
VMEM_LIMIT = 56 * 1024 * 1024
FFN_VMEM_LIMIT = 60 * 1024 * 1024

bf16 = jnp.bfloat16
f32 = jnp.float32


def _rms(x, g):
    return x * lax.rsqrt(jnp.mean(x * x, axis=-1, keepdims=True) + EPS) * g


def _dot(a, b):
    return jnp.dot(a, b, preferred_element_type=f32)


def _dot_nt(a, b):
    return lax.dot_general(a, b, (((1,), (1,)), ((), ())), preferred_element_type=f32)


def _mod_row(tm):
    return lambda t: jnp.minimum((t * tm) // SEQ, BATCH)


def _params(n_axes, vmem=VMEM_LIMIT):
    return pltpu.CompilerParams(
        dimension_semantics=("arbitrary",) * n_axes, vmem_limit_bytes=vmem)


def _resident(shape, index_map):
    return pl.BlockSpec(shape, index_map, pipeline_mode=pl.Buffered(1))


def _mod_kernel(c_ref, w_ref, b_ref, o_ref):
    cc = c_ref[...]
    s = (cc * jax.nn.sigmoid(cc)).astype(bf16)
    o_ref[0] = _dot(s, w_ref[0].astype(bf16)) + b_ref[0]


def _modulation(c_all, w_mod, b_mod):
    n = N_MOD * D
    return pl.pallas_call(
        _mod_kernel,
        grid=(DEPTH, n // TN_MOD),
        in_specs=[
            pl.BlockSpec((MOD_ROWS, D), lambda i, j: (0, 0)),
            pl.BlockSpec((1, D, TN_MOD), lambda i, j: (i, 0, j)),
            pl.BlockSpec((1, 1, TN_MOD), lambda i, j: (i, 0, j)),
        ],
        out_specs=pl.BlockSpec((1, MOD_ROWS, TN_MOD), lambda i, j: (i, 0, j)),
        out_shape=jax.ShapeDtypeStruct((DEPTH, MOD_ROWS, n), f32),
        compiler_params=_params(2),
        name="modulation",
    )(c_all, w_mod, b_mod.reshape(DEPTH, 1, n))


def _gmlp_kernel(x_ref, *rest, split_input):
    if split_input:
        c_ref, *rest = rest
        x = jnp.where(pl.program_id(0) < T_LAT // TM_MIX, x_ref[...], c_ref[...])
    else:
        x = x_ref[...]
    mod_ref, ng_ref, win_ref, vg_ref, ws_ref, bs_ref, wout_ref, o_ref = rest
    mod = mod_ref[0, 0]
    ng = ng_ref[0]
    h = _rms(x, ng[0:1]) * (1.0 + mod[1:2]) + mod[0:1]
    z = _dot(h.astype(bf16), win_ref[0])
    z = 0.5 * z * (1.0 + lax.erf(z * np.float32(np.sqrt(0.5))))
    u = z[:, :D]
    v = z[:, D:]
    mu = jnp.mean(v, axis=-1, keepdims=True)
    vc = v - mu
    var = jnp.mean(vc * vc, axis=-1, keepdims=True)
    vn = (vc * lax.rsqrt(var + EPS) * vg_ref[0]).astype(bf16)
    n_chunks = TM_MIX // CHUNK
    cols = []
    for g in range(GROUPS):
        c0 = g * GROUP_DIM
        vg = jnp.concatenate(
            [vn[ck * CHUNK:(ck + 1) * CHUNK, c0:c0 + GROUP_DIM] for ck in range(n_chunks)], axis=1)
        sg = _dot(ws_ref[0, g], vg)
        cols.append(jnp.concatenate(
            [sg[:, ck * GROUP_DIM:(ck + 1) * GROUP_DIM] for ck in range(n_chunks)], axis=0))
    s = jnp.concatenate(cols, axis=1) + jnp.concatenate([bs_ref[0]] * n_chunks, axis=0)
    y = _dot((u * s).astype(bf16), wout_ref[0])
    o_ref[...] = x + mod[2:3] * _rms(y, ng[1:2])


def _gmlp_layer(xs, mod, norm_g, w_in, v_g, w_s, b_full, w_out, i, j, n_tok):
    tm = TM_MIX
    row = _mod_row(tm)
    n_lat = T_LAT // tm
    if len(xs) == 2:
        x_specs = [pl.BlockSpec((tm, D), lambda t: (jnp.minimum(t, n_lat - 1), 0)),
                   pl.BlockSpec((tm, D), lambda t: (jnp.maximum(t - n_lat, 0), 0))]
    else:
        x_specs = [pl.BlockSpec((tm, D), lambda t: (t, 0))]
    return pl.pallas_call(
        functools.partial(_gmlp_kernel, split_input=len(xs) == 2),
        grid=(n_tok // tm,),
        in_specs=x_specs + [
            pl.BlockSpec((1, 1, N_MOD, D), lambda t: (i, row(t), 0, 0)),
            pl.BlockSpec((1, 4, D), lambda t: (i, 0, 0)),
            _resident((1, D, 2 * D), lambda t: (j, 0, 0)),
            pl.BlockSpec((1, 1, D), lambda t: (j, 0, 0)),
            _resident((1, GROUPS, CHUNK, CHUNK), lambda t: (j, 0, 0, 0)),
            _resident((1, CHUNK, D), lambda t: (j, 0, 0)),
            _resident((1, D, D), lambda t: (j, 0, 0)),
        ],
        out_specs=pl.BlockSpec((tm, D), lambda t: (t, 0)),
        out_shape=jax.ShapeDtypeStruct((n_tok, D), f32),
        compiler_params=_params(1),
        name="gmlp_mixer",
    )(*xs, mod, norm_g, w_in, v_g, w_s, b_full, w_out)


def _rope(xt, cos, sa, sb):
    return xt * cos + pltpu.roll(xt, 16, 1) * sa + pltpu.roll(xt, LANES - 16, 1) * sb


def _qkv_kernel(x_ref, mod_ref, ng_ref, w_ref, cos_ref, sa_ref, sb_ref, qt_ref, k_ref, vt_ref):
    x = x_ref[...]
    mod = mod_ref[0, 0]
    h = _rms(x, ng_ref[0][0:1]) * (1.0 + mod[1:2]) + mod[0:1]
    qkv = _dot(h.astype(bf16), w_ref[0])
    cos, sa, sb = cos_ref[...], sa_ref[...], sb_ref[...]
    scale = np.float32(HEAD_DIM ** -0.5 * LOG2E)
    n_blk = x.shape[0] // ATT_BLOCK
    for p in range(D // LANES):
        c0 = p * LANES
        qt = _rope(qkv[:, c0:c0 + LANES], cos, sa, sb) * scale
        for r in range(n_blk):
            r0 = r * ATT_BLOCK
            qt_ref[c0:c0 + LANES, r0:r0 + ATT_BLOCK] = qt[r0:r0 + ATT_BLOCK].T.astype(bf16)
    lane = lax.broadcasted_iota(jnp.int32, (x.shape[0], LANES), 1)
    lo = lane < HEAD_DIM
    for m in range(N_KV_HEADS // 2):
        kt = _rope(qkv[:, D + m * LANES:D + (m + 1) * LANES], cos, sa, sb)
        vt = qkv[:, D + N_KV_HEADS * HEAD_DIM + m * LANES:D + N_KV_HEADS * HEAD_DIM + (m + 1) * LANES]
        base = 2 * m * 2 * PAIR
        sw = pltpu.roll(kt, HEAD_DIM, 1)
        k_ref[:, base:base + PAIR] = jnp.where(lo, kt, 0.0).astype(bf16)
        k_ref[:, base + PAIR:base + 2 * PAIR] = jnp.where(lo, 0.0, sw).astype(bf16)
        k_ref[:, base + 2 * PAIR:base + 3 * PAIR] = jnp.where(lo, sw, 0.0).astype(bf16)
        k_ref[:, base + 3 * PAIR:base + 4 * PAIR] = jnp.where(lo, 0.0, kt).astype(bf16)
        sw = pltpu.roll(vt, HEAD_DIM, 1)
        one_hi = (lane == HEAD_DIM).astype(f32)
        one_lo = (lane == 0).astype(f32)
        parts = (jnp.where(lo, vt, one_hi), jnp.where(lo, one_lo, sw),
                 jnp.where(lo, sw, one_hi), jnp.where(lo, one_lo, vt))
        for e, part in enumerate(parts):
            for r in range(n_blk):
                r0 = r * ATT_BLOCK
                vt_ref[r, base + e * PAIR:base + (e + 1) * PAIR, :] = part[r0:r0 + ATT_BLOCK].T.astype(bf16)


def _qkv_layer(x_all, mod, norm_g, w_qkv, rope, i, j):
    tm = TM_MIX
    row = _mod_row(tm)
    n_lat_tiles = SEQ // tm
    n_blk = tm // ATT_BLOCK
    rope_row = lambda t: (jnp.where(t < T_LAT // tm, t % n_lat_tiles, n_lat_tiles), 0)
    n_qkv = w_qkv.shape[-1]
    return pl.pallas_call(
        _qkv_kernel,
        grid=(T_ALL // tm,),
        in_specs=[
            pl.BlockSpec((tm, D), lambda t: (t, 0)),
            pl.BlockSpec((1, 1, N_MOD, D), lambda t: (i, row(t), 0, 0)),
            pl.BlockSpec((1, 4, D), lambda t: (i, 0, 0)),
            _resident((1, D, n_qkv), lambda t: (j, 0, 0)),
            pl.BlockSpec((tm, LANES), rope_row),
            pl.BlockSpec((tm, LANES), rope_row),
            pl.BlockSpec((tm, LANES), rope_row),
        ],
        out_specs=[
            pl.BlockSpec((D, tm), lambda t: (0, t)),
            pl.BlockSpec((tm, KV_EXT), lambda t: (t, 0)),
            pl.BlockSpec((n_blk, KV_EXT, ATT_BLOCK), lambda t: (t, 0, 0)),
        ],
        out_shape=[
            jax.ShapeDtypeStruct((D, T_ALL), bf16),
            jax.ShapeDtypeStruct((T_ALL, KV_EXT), bf16),
            jax.ShapeDtypeStruct((T_ALL // ATT_BLOCK, KV_EXT, ATT_BLOCK), bf16),
        ],
        compiler_params=_params(1),
        name="qkv_rope",
    )(x_all, mod, norm_g, w_qkv, *rope)


def _attn_kernel(sink_ref, qt_ref, kc_ref, vct_ref, *rest, layer, with_band):
    n_pairs = KV_GROUP // 2
    n_cols = n_pairs * ATT_BLOCK
    if with_band:
        kl_ref, vlt_ref, o_ref = rest
        n = pl.program_id(1)
        blk0 = jnp.clip(n - 1, 0, (SEQ - BAND) // ATT_BLOCK)
        s0 = pl.multiple_of(blk0 * ATT_BLOCK, ATT_BLOCK)
        rel = n * ATT_BLOCK - s0
        kj = lax.broadcasted_iota(jnp.int32, (BAND, n_cols), 0)
        qi = lax.broadcasted_iota(jnp.int32, (BAND, n_cols), 1) & (ATT_BLOCK - 1)
        dist = qi - kj + rel
        bias = jnp.where((dist <= WINDOW) & (dist >= -WINDOW), 0.0, NEG_INF).astype(f32)
    else:
        _, o_ref = rest
    even_rows = lax.broadcasted_iota(jnp.int32, (PAIR, n_cols), 0) < HEAD_DIM
    for kh in range(N_KV_HEADS):
        qt = jnp.concatenate(
            [qt_ref[(kh * n_pairs + pp) * PAIR:(kh * n_pairs + pp + 1) * PAIR, :] for pp in range(n_pairs)],
            axis=1)
        halves = []
        for par in range(2):
            c0 = (2 * kh + par) * PAIR
            sink = jnp.concatenate(
                [jnp.full((1, ATT_BLOCK), sink_ref[layer, kh * KV_GROUP + 2 * pp + par] * LOG2E, f32)
                 for pp in range(n_pairs)], axis=1)
            s_ctx = _dot(kc_ref[:, c0:c0 + PAIR], qt)
            m = jnp.maximum(jnp.max(s_ctx, axis=0, keepdims=True), sink)
            if with_band:
                s_band = _dot(kl_ref[pl.ds(s0, BAND), c0:c0 + PAIR], qt) + bias
                m = jnp.maximum(m, jnp.max(s_band, axis=0, keepdims=True))
            vct = jnp.concatenate(
                [vct_ref[r, c0:c0 + PAIR, :] for r in range(CTX_LEN // ATT_BLOCK)], axis=1)
            o = _dot(vct, jnp.exp2(s_ctx - m).astype(bf16))
            if with_band:
                vlt = jnp.concatenate(
                    [vlt_ref[blk0 + r, c0:c0 + PAIR, :] for r in range(BAND // ATT_BLOCK)], axis=1)
                o = o + _dot(vlt, jnp.exp2(s_band - m).astype(bf16))
            den_row = HEAD_DIM if par == 0 else 0
            den = o[den_row:den_row + 1, :] + jnp.exp2(sink - m)
            halves.append(o * (1.0 / den))
        acc = jnp.where(even_rows, halves[0], halves[1])
        for pp in range(n_pairs):
            c0 = (kh * n_pairs + pp) * PAIR
            o_ref[:, c0:c0 + PAIR] = acc[:, pp * ATT_BLOCK:(pp + 1) * ATT_BLOCK].T.astype(bf16)


def _attention(sink, qt, k_ext, vt_ext, j, n_rows, o_lat=None):
    with_band = o_lat is None
    ctx_blk = T_LAT // CTX_LEN
    ctx_sub = CTX_LEN // ATT_BLOCK
    smem = pl.BlockSpec(memory_space=pltpu.SMEM)
    kc_spec = pl.BlockSpec((CTX_LEN, KV_EXT), lambda b, n: (ctx_blk + b, 0))
    vc_spec = pl.BlockSpec((ctx_sub, KV_EXT, ATT_BLOCK), lambda b, n: (ctx_blk + b, 0, 0))
    if with_band:
        nb = SEQ // ATT_BLOCK
        q_spec = pl.BlockSpec((D, ATT_BLOCK), lambda b, n: (0, b * nb + n))
        kl_spec = pl.BlockSpec((SEQ, KV_EXT), lambda b, n: (b, 0))
        vl_spec = pl.BlockSpec((nb, KV_EXT, ATT_BLOCK), lambda b, n: (b, 0, 0))
        in_specs = [smem, q_spec, kc_spec, vc_spec, kl_spec, vl_spec]
        args = (sink, qt, k_ext, vt_ext, k_ext, vt_ext)
        first_blk = 0
        aliases = {}
    else:
        nb = ctx_sub
        first_blk = T_LAT // ATT_BLOCK
        q_spec = pl.BlockSpec((D, ATT_BLOCK), lambda b, n: (0, first_blk + b * nb + n))
        in_specs = [smem, q_spec, kc_spec, vc_spec, pl.BlockSpec(memory_space=pl.ANY)]
        args = (sink, qt, k_ext, vt_ext, o_lat)
        aliases = {len(args) - 1: 0}
    return pl.pallas_call(
        functools.partial(_attn_kernel, layer=j, with_band=with_band),
        grid=(BATCH, nb),
        in_specs=in_specs,
        out_specs=pl.BlockSpec((ATT_BLOCK, D), lambda b, n: (first_blk + b * nb + n, 0)),
        out_shape=jax.ShapeDtypeStruct((n_rows, D), bf16),
        input_output_aliases=aliases,
        compiler_params=_params(2),
        name="attn_band" if with_band else "attn_ctx",
    )(*args)


def _proj_kernel(x_ref, o_ref_in, mod_ref, ng_ref, w_ref, out_ref):
    y = _dot(o_ref_in[...], w_ref[0])
    out_ref[...] = x_ref[...] + mod_ref[0, 0][2:3] * _rms(y, ng_ref[0][1:2])


def _proj_layer(x_all, o_all, mod, norm_g, w_o, i, j, n_tok):
    tm = TM_MIX
    row = _mod_row(tm)
    return pl.pallas_call(
        _proj_kernel,
        grid=(n_tok // tm,),
        in_specs=[
            pl.BlockSpec((tm, D), lambda t: (t, 0)),
            pl.BlockSpec((tm, D), lambda t: (t, 0)),
            pl.BlockSpec((1, 1, N_MOD, D), lambda t: (i, row(t), 0, 0)),
            pl.BlockSpec((1, 4, D), lambda t: (i, 0, 0)),
            _resident((1, D, D), lambda t: (j, 0, 0)),
        ],
        out_specs=pl.BlockSpec((tm, D), lambda t: (t, 0)),
        out_shape=jax.ShapeDtypeStruct((n_tok, D), f32),
        compiler_params=_params(1),
        name="attn_proj",
    )(x_all, o_all, mod, norm_g, w_o)


def _row_loop(n_rows, body):
    def step(r, carry):
        body(pl.ds(pl.multiple_of(r * ROW_CHUNK, ROW_CHUNK), ROW_CHUNK))
        return carry
    lax.fori_loop(0, n_rows // ROW_CHUNK, step, 0, unroll=ROW_UNROLL)


def _ffn_kernel(x_ref, mod_ref, ng_ref, wg_ref, wu_ref, wd_ref, o_ref, h_scr, acc_scr):
    hstep = pl.program_id(1)
    tm = x_ref.shape[0]

    @pl.when(hstep == 0)
    def _():
        mod = mod_ref[0, 0]
        gain = ng_ref[0][2:3] * (1.0 + mod[4:5])
        shift = mod[3:4]

        def body(rows):
            xr = x_ref[rows, :]
            inv = lax.rsqrt(jnp.mean(xr * xr, axis=-1, keepdims=True) + EPS)
            h_scr[rows, :] = (xr * inv * gain + shift).astype(bf16)
            acc_scr[rows, :] = jnp.zeros((ROW_CHUNK, D), f32)
        _row_loop(tm, body)

    h = h_scr[...]
    g = _dot(h, wg_ref[0].astype(bf16))
    u = _dot(h, wu_ref[0].astype(bf16))
    a = (g * jax.nn.sigmoid(g) * u).astype(bf16)
    wd = wd_ref[0].astype(bf16)
    for c in range(D // FFN_DOWN_COLS):
        cols = slice(c * FFN_DOWN_COLS, (c + 1) * FFN_DOWN_COLS)
        acc_scr[:, cols] += _dot(a, wd[:, cols])

    @pl.when(hstep == pl.num_programs(1) - 1)
    def _():
        gain = mod_ref[0, 0][5:6] * ng_ref[0][3:4]

        def body(rows):
            y = acc_scr[rows, :]
            inv = lax.rsqrt(jnp.mean(y * y, axis=-1, keepdims=True) + EPS)
            o_ref[rows, :] = x_ref[rows, :] + y * inv * gain
        _row_loop(tm, body)


def _ffn_layer(x_all, mod, norm_g, w_gu, w_down, i, n_tok):
    tm, th = TM_FFN, TH_FFN
    row = _mod_row(tm)
    nh = FFN_HIDDEN // th
    return pl.pallas_call(
        _ffn_kernel,
        grid=(n_tok // tm, nh),
        in_specs=[
            pl.BlockSpec((tm, D), lambda t, h: (t, 0)),
            pl.BlockSpec((1, 1, N_MOD, D), lambda t, h: (i, row(t), 0, 0)),
            pl.BlockSpec((1, 4, D), lambda t, h: (i, 0, 0)),
            pl.BlockSpec((1, D, th), lambda t, h: (i, 0, h)),
            pl.BlockSpec((1, D, th), lambda t, h: (i, 0, nh + h)),
            pl.BlockSpec((1, th, D), lambda t, h: (i, h, 0)),
        ],
        out_specs=pl.BlockSpec((tm, D), lambda t, h: (t, 0)),
        out_shape=jax.ShapeDtypeStruct((n_tok, D), f32),
        scratch_shapes=[pltpu.VMEM((tm, D), bf16), pltpu.VMEM((tm, D), f32)],
        compiler_params=_params(2, FFN_VMEM_LIMIT),
        name="swiglu_ffn",
    )(x_all, mod, norm_g, w_gu, w_gu, w_down)


def _rope_tables():
    rows = SEQ // GRID_W
    pos = np.stack([np.repeat(np.arange(rows), GRID_W), np.tile(np.arange(GRID_W), rows)], -1)
    n_freq = HEAD_DIM // 4
    inv = jnp.asarray(ROPE_THETA, f32) ** (-jnp.arange(n_freq, dtype=f32) / n_freq)
    ang = jnp.asarray(pos, f32)[:, :, None] * inv
    cos, sin = jnp.cos(ang), jnp.sin(ang)
    zero = jnp.zeros_like(sin)
    cos_h = jnp.concatenate([cos[:, 0], cos[:, 0], cos[:, 1], cos[:, 1]], -1)
    sa_h = jnp.concatenate([zero[:, 0], sin[:, 0], zero[:, 1], sin[:, 1]], -1)
    sb_h = jnp.concatenate([-sin[:, 0], zero[:, 0], -sin[:, 1], zero[:, 1]], -1)
    ident = (jnp.ones((TM_MIX, HEAD_DIM), f32), jnp.zeros((TM_MIX, HEAD_DIM), f32),
             jnp.zeros((TM_MIX, HEAD_DIM), f32))
    return tuple(jnp.tile(jnp.concatenate([t, e], 0), (1, LANES // HEAD_DIM))
                 for t, e in zip((cos_h, sa_h, sb_h), ident))


def kernel(x, c, ctx, c_ctx, w_mod, b_mod, norm_g, a_w_in, a_v_norm_g, a_w_s, a_b_s, a_w_out,
           b_w_qkv, b_sink, b_w_o, f_w_gu, f_w_down):
    assert x.shape == (BATCH, SEQ, D) and ctx.shape == (BATCH, CTX_LEN, D)
    xs = (x.reshape(T_LAT, D), ctx.reshape(T_CTX, D))
    c_all = jnp.concatenate(
        [c, c_ctx[None], jnp.zeros((MOD_ROWS - BATCH - 1, D), f32)], axis=0)
    mod = _modulation(c_all, w_mod, b_mod).reshape(DEPTH, MOD_ROWS, N_MOD, D)

    w_in, w_out = a_w_in.astype(bf16), a_w_out.astype(bf16)
    w_s = a_w_s.astype(bf16)
    b_full = jnp.repeat(jnp.swapaxes(a_b_s, 1, 2), GROUP_DIM, axis=2)
    v_g = a_v_norm_g.reshape(-1, 1, D)
    w_qkv, w_o = b_w_qkv.astype(bf16), b_w_o.astype(bf16)
    w_gu, w_down = f_w_gu, f_w_down
    rope = _rope_tables()

    for i in range(DEPTH):
        last = i == DEPTH - 1
        j = i // 2
        n_tok = T_LAT if last else T_ALL
        if i % 2 == 0:
            x_all = _gmlp_layer(xs if i == 0 else (x_all,), mod, norm_g, w_in, v_g, w_s, b_full, w_out,
                                i, j, n_tok)
        else:
            qt, k_ext, vt_ext = _qkv_layer(x_all, mod, norm_g, w_qkv, rope, i, j)
            o = _attention(b_sink, qt, k_ext, vt_ext, j, n_tok)
            if not last:
                o = _attention(b_sink, qt, k_ext, vt_ext, j, n_tok, o_lat=o)
            x_all = _proj_layer(x_all, o, mod, norm_g, w_o, i, j, n_tok)
        x_all = _ffn_layer(x_all, mod, norm_g, w_gu, w_down, i, n_tok)
    return x_all.reshape(BATCH, SEQ, D)
```

```python
import functools

import jax
import jax.numpy as jnp
import numpy as np
from jax import lax
from jax.experimental import pallas as pl
from jax.experimental.pallas import tpu as pltpu

D = 2048
BATCH = 4
SEQ = 2048
DEPTH = 4
GRID_W = 64
CTX_LEN = 256
CHUNK = 128
GROUP_DIM = 128
GROUPS = D // GROUP_DIM
HEAD_DIM = 64
N_Q_HEADS = D // HEAD_DIM
KV_GROUP = 8
N_KV_HEADS = N_Q_HEADS // KV_GROUP
WINDOW = 128
ATT_BLOCK = 128
BAND = 3 * ATT_BLOCK
ROPE_THETA = 10000.0
FFN_HIDDEN = 5632
N_MOD = 6
EPS = 1e-6
NEG_INF = -1e30

T_LAT = BATCH * SEQ
T_CTX = BATCH * CTX_LEN
T_ALL = T_LAT + T_CTX
MOD_ROWS = 8
LANES = 128
LOG2E = float(np.log2(np.e))
PAIR = 2 * HEAD_DIM
KV_EXT = N_KV_HEADS * 2 * PAIR

TM_MIX = 256
TM_FFN = 1024
TH_FFN = 256
FFN_DOWN_COLS = 512
ROW_CHUNK = 16
ROW_UNROLL = 16
ATTN_LOOKAHEAD = 2
TN_MOD = 1024
VMEM_LIMIT = 56 * 1024 * 1024
BIG_VMEM_LIMIT = 60 * 1024 * 1024

bf16 = jnp.bfloat16
f32 = jnp.float32


def _rms(x, g):
    return x * lax.rsqrt(jnp.mean(x * x, axis=-1, keepdims=True) + EPS) * g


def _dot(a, b):
    return jnp.dot(a, b, preferred_element_type=f32)


def _dot_nt(a, b):
    return lax.dot_general(a, b, (((1,), (1,)), ((), ())), preferred_element_type=f32)


def _mod_row(tm):
    return lambda t: jnp.minimum((t * tm) // SEQ, BATCH)


def _params(n_axes, vmem=VMEM_LIMIT):
    return pltpu.CompilerParams(
        dimension_semantics=("arbitrary",) * n_axes, vmem_limit_bytes=vmem)


def _resident(shape, index_map):
    return pl.BlockSpec(shape, index_map, pipeline_mode=pl.Buffered(1))


def _mod_kernel(c_ref, w_ref, b_ref, o_ref):
    cc = c_ref[...]
    s = (cc * jax.nn.sigmoid(cc)).astype(bf16)
    o_ref[0] = _dot(s, w_ref[0].astype(bf16)) + b_ref[0]


def _modulation(c_all, w_mod, b_mod):
    n = N_MOD * D
    return pl.pallas_call(
        _mod_kernel,
        grid=(DEPTH, n // TN_MOD),
        in_specs=[
            pl.BlockSpec((MOD_ROWS, D), lambda i, j: (0, 0)),
            pl.BlockSpec((1, D, TN_MOD), lambda i, j: (i, 0, j)),
            pl.BlockSpec((1, 1, TN_MOD), lambda i, j: (i, 0, j)),
        ],
        out_specs=pl.BlockSpec((1, MOD_ROWS, TN_MOD), lambda i, j: (i, 0, j)),
        out_shape=jax.ShapeDtypeStruct((DEPTH, MOD_ROWS, n), f32),
        compiler_params=_params(2),
        name="modulation",
    )(c_all, w_mod, b_mod.reshape(DEPTH, 1, n))


def _gmlp_kernel(x_ref, *rest, split_input):
    if split_input:
        c_ref, *rest = rest
        x = jnp.where(pl.program_id(0) < T_LAT // TM_MIX, x_ref[...], c_ref[...])
    else:
        x = x_ref[...]
    mod_ref, ng_ref, win_ref, vg_ref, ws_ref, bs_ref, wout_ref, o_ref = rest
    mod = mod_ref[0, 0]
    ng = ng_ref[0]
    h = _rms(x, ng[0:1]) * (1.0 + mod[1:2]) + mod[0:1]
    z = _dot(h.astype(bf16), win_ref[0])
    z = 0.5 * z * (1.0 + lax.erf(z * np.float32(np.sqrt(0.5))))
    u = z[:, :D]
    v = z[:, D:]
    mu = jnp.mean(v, axis=-1, keepdims=True)
    vc = v - mu
    var = jnp.mean(vc * vc, axis=-1, keepdims=True)
    vn = (vc * lax.rsqrt(var + EPS) * vg_ref[0]).astype(bf16)
    n_chunks = TM_MIX // CHUNK
    cols = []
    for g in range(GROUPS):
        c0 = g * GROUP_DIM
        vg = jnp.concatenate(
            [vn[ck * CHUNK:(ck + 1) * CHUNK, c0:c0 + GROUP_DIM] for ck in range(n_chunks)], axis=1)
        sg = _dot(ws_ref[0, g], vg)
        cols.append(jnp.concatenate(
            [sg[:, ck * GROUP_DIM:(ck + 1) * GROUP_DIM] for ck in range(n_chunks)], axis=0))
    s = jnp.concatenate(cols, axis=1) + jnp.concatenate([bs_ref[0]] * n_chunks, axis=0)
    y = _dot((u * s).astype(bf16), wout_ref[0])
    o_ref[...] = x + mod[2:3] * _rms(y, ng[1:2])


def _gmlp_layer(xs, mod, norm_g, w_in, v_g, w_s, b_full, w_out, i, j, n_tok):
    tm = TM_MIX
    row = _mod_row(tm)
    n_lat = T_LAT // tm
    if len(xs) == 2:
        x_specs = [pl.BlockSpec((tm, D), lambda t: (jnp.minimum(t, n_lat - 1), 0)),
                   pl.BlockSpec((tm, D), lambda t: (jnp.maximum(t - n_lat, 0), 0))]
    else:
        x_specs = [pl.BlockSpec((tm, D), lambda t: (t, 0))]
    return pl.pallas_call(
        functools.partial(_gmlp_kernel, split_input=len(xs) == 2),
        grid=(n_tok // tm,),
        in_specs=x_specs + [
            pl.BlockSpec((1, 1, N_MOD, D), lambda t: (i, row(t), 0, 0)),
            pl.BlockSpec((1, 4, D), lambda t: (i, 0, 0)),
            _resident((1, D, 2 * D), lambda t: (j, 0, 0)),
            pl.BlockSpec((1, 1, D), lambda t: (j, 0, 0)),
            _resident((1, GROUPS, CHUNK, CHUNK), lambda t: (j, 0, 0, 0)),
            _resident((1, CHUNK, D), lambda t: (j, 0, 0)),
            _resident((1, D, D), lambda t: (j, 0, 0)),
        ],
        out_specs=pl.BlockSpec((tm, D), lambda t: (t, 0)),
        out_shape=jax.ShapeDtypeStruct((n_tok, D), f32),
        compiler_params=_params(1),
        name="gmlp_mixer",
    )(*xs, mod, norm_g, w_in, v_g, w_s, b_full, w_out)


def _rope(xt, cos, sa, sb):
    return xt * cos + pltpu.roll(xt, 16, 1) * sa + pltpu.roll(xt, LANES - 16, 1) * sb


def _qkv_kernel(x_ref, mod_ref, ng_ref, w_ref, cos_ref, sa_ref, sb_ref, qt_ref, k_ref, vt_ref):
    x = x_ref[...]
    mod = mod_ref[0, 0]
    h = _rms(x, ng_ref[0][0:1]) * (1.0 + mod[1:2]) + mod[0:1]
    qkv = _dot(h.astype(bf16), w_ref[0])
    cos, sa, sb = cos_ref[...], sa_ref[...], sb_ref[...]
    scale = np.float32(HEAD_DIM ** -0.5 * LOG2E)
    n_blk = x.shape[0] // ATT_BLOCK
    for p in range(D // LANES):
        c0 = p * LANES
        qt = _rope(qkv[:, c0:c0 + LANES], cos, sa, sb) * scale
        for r in range(n_blk):
            r0 = r * ATT_BLOCK
            qt_ref[c0:c0 + LANES, r0:r0 + ATT_BLOCK] = qt[r0:r0 + ATT_BLOCK].T.astype(bf16)
    lane = lax.broadcasted_iota(jnp.int32, (x.shape[0], LANES), 1)
    lo = lane < HEAD_DIM
    for m in range(N_KV_HEADS // 2):
        kt = _rope(qkv[:, D + m * LANES:D + (m + 1) * LANES], cos, sa, sb)
        vt = qkv[:, D + N_KV_HEADS * HEAD_DIM + m * LANES:D + N_KV_HEADS * HEAD_DIM + (m + 1) * LANES]
        base = 2 * m * 2 * PAIR
        sw = pltpu.roll(kt, HEAD_DIM, 1)
        k_ref[:, base:base + PAIR] = jnp.where(lo, kt, 0.0).astype(bf16)
        k_ref[:, base + PAIR:base + 2 * PAIR] = jnp.where(lo, 0.0, sw).astype(bf16)
        k_ref[:, base + 2 * PAIR:base + 3 * PAIR] = jnp.where(lo, sw, 0.0).astype(bf16)
        k_ref[:, base + 3 * PAIR:base + 4 * PAIR] = jnp.where(lo, 0.0, kt).astype(bf16)
        sw = pltpu.roll(vt, HEAD_DIM, 1)
        one_hi = (lane == HEAD_DIM).astype(f32)
        one_lo = (lane == 0).astype(f32)
        parts = (jnp.where(lo, vt, one_hi), jnp.where(lo, one_lo, sw),
                 jnp.where(lo, sw, one_hi), jnp.where(lo, one_lo, vt))
        for e, part in enumerate(parts):
            for r in range(n_blk):
                r0 = r * ATT_BLOCK
                vt_ref[r, base + e * PAIR:base + (e + 1) * PAIR, :] = part[r0:r0 + ATT_BLOCK].T.astype(bf16)


def _qkv_layer(x_all, mod, norm_g, w_qkv, rope, i, j):
    tm = TM_MIX
    row = _mod_row(tm)
    n_lat_tiles = SEQ // tm
    n_blk = tm // ATT_BLOCK
    rope_row = lambda t: (jnp.where(t < T_LAT // tm, t % n_lat_tiles, n_lat_tiles), 0)
    n_qkv = w_qkv.shape[-1]
    return pl.pallas_call(
        _qkv_kernel,
        grid=(T_ALL // tm,),
        in_specs=[
            pl.BlockSpec((tm, D), lambda t: (t, 0)),
            pl.BlockSpec((1, 1, N_MOD, D), lambda t: (i, row(t), 0, 0)),
            pl.BlockSpec((1, 4, D), lambda t: (i, 0, 0)),
            _resident((1, D, n_qkv), lambda t: (j, 0, 0)),
            pl.BlockSpec((tm, LANES), rope_row),
            pl.BlockSpec((tm, LANES), rope_row),
            pl.BlockSpec((tm, LANES), rope_row),
        ],
        out_specs=[
            pl.BlockSpec((D, tm), lambda t: (0, t)),
            pl.BlockSpec((tm, KV_EXT), lambda t: (t, 0)),
            pl.BlockSpec((n_blk, KV_EXT, ATT_BLOCK), lambda t: (t, 0, 0)),
        ],
        out_shape=[
            jax.ShapeDtypeStruct((D, T_ALL), bf16),
            jax.ShapeDtypeStruct((T_ALL, KV_EXT), bf16),
            jax.ShapeDtypeStruct((T_ALL // ATT_BLOCK, KV_EXT, ATT_BLOCK), bf16),
        ],
        compiler_params=_params(1),
        name="qkv_rope",
    )(x_all, mod, norm_g, w_qkv, *rope)


def _attn_kernel(sink_ref, qt_ref, kc_ref, vct_ref, *rest, layer, with_band):
    n_pairs = KV_GROUP // 2
    n_cols = n_pairs * ATT_BLOCK
    if with_band:
        bias_ref, kl_ref, vlt_ref, o_ref = rest
        blk0 = jnp.clip(pl.program_id(1) - 1, 0, (SEQ - BAND) // ATT_BLOCK)
        s0 = pl.multiple_of(blk0 * ATT_BLOCK, ATT_BLOCK)
    else:
        _, o_ref = rest
    even_rows = lax.broadcasted_iota(jnp.int32, (PAIR, n_cols), 0) < HEAD_DIM

    def scores(kh, par):
        qt = jnp.concatenate(
            [qt_ref[(kh * n_pairs + pp) * PAIR:(kh * n_pairs + pp + 1) * PAIR, :] for pp in range(n_pairs)],
            axis=1)
        c0 = (2 * kh + par) * PAIR
        s_ctx = _dot(kc_ref[:, c0:c0 + PAIR], qt)
        s_band = _dot(kl_ref[pl.ds(s0, BAND), c0:c0 + PAIR], qt) + bias_ref[0] if with_band else None
        return s_ctx, s_band

    def weighted_values(kh, par, s_ctx, s_band):
        c0 = (2 * kh + par) * PAIR
        sink = jnp.concatenate(
            [jnp.full((1, ATT_BLOCK), sink_ref[layer, kh * KV_GROUP + 2 * pp + par] * LOG2E, f32)
             for pp in range(n_pairs)], axis=1)
        m = jnp.maximum(jnp.max(s_ctx, axis=0, keepdims=True), sink)
        if with_band:
            m = jnp.maximum(m, jnp.max(s_band, axis=0, keepdims=True))
        vct = jnp.concatenate(
            [vct_ref[r, c0:c0 + PAIR, :] for r in range(CTX_LEN // ATT_BLOCK)], axis=1)
        o = _dot(vct, jnp.exp2(s_ctx - m).astype(bf16))
        if with_band:
            vlt = jnp.concatenate(
                [vlt_ref[blk0 + r, c0:c0 + PAIR, :] for r in range(BAND // ATT_BLOCK)], axis=1)
            o = o + _dot(vlt, jnp.exp2(s_band - m).astype(bf16))
        den_row = HEAD_DIM if par == 0 else 0
        den = o[den_row:den_row + 1, :] + jnp.exp2(sink - m)
        return o * (1.0 / den)

    order = [(kh, par) for kh in range(N_KV_HEADS) for par in range(2)]
    pending = [scores(*order[it]) for it in range(ATTN_LOOKAHEAD)]
    halves = []
    for it, (kh, par) in enumerate(order):
        if it + ATTN_LOOKAHEAD < len(order):
            pending.append(scores(*order[it + ATTN_LOOKAHEAD]))
        halves.append(weighted_values(kh, par, *pending.pop(0)))
        if par == 1:
            acc = jnp.where(even_rows, halves[0], halves[1])
            halves = []
            for pp in range(n_pairs):
                c0 = (kh * n_pairs + pp) * PAIR
                o_ref[:, c0:c0 + PAIR] = acc[:, pp * ATT_BLOCK:(pp + 1) * ATT_BLOCK].T.astype(bf16)


def _band_bias():
    kj = np.arange(BAND)[:, None]
    qi = np.arange(KV_GROUP // 2 * ATT_BLOCK)[None, :] % ATT_BLOCK
    tabs = [np.where(np.abs(qi + rel - kj) <= WINDOW, 0.0, NEG_INF) for rel in (0, ATT_BLOCK, 2 * ATT_BLOCK)]
    return jnp.asarray(np.stack(tabs), f32)


def _attention(sink, qt, k_ext, vt_ext, j, n_rows, o_lat=None):
    with_band = o_lat is None
    ctx_blk = T_LAT // CTX_LEN
    ctx_sub = CTX_LEN // ATT_BLOCK
    smem = pl.BlockSpec(memory_space=pltpu.SMEM)
    kc_spec = pl.BlockSpec((CTX_LEN, KV_EXT), lambda b, n: (ctx_blk + b, 0))
    vc_spec = pl.BlockSpec((ctx_sub, KV_EXT, ATT_BLOCK), lambda b, n: (ctx_blk + b, 0, 0))
    if with_band:
        nb = SEQ // ATT_BLOCK
        q_spec = pl.BlockSpec((D, ATT_BLOCK), lambda b, n: (0, b * nb + n))
        kl_spec = pl.BlockSpec((SEQ, KV_EXT), lambda b, n: (b, 0))
        vl_spec = pl.BlockSpec((nb, KV_EXT, ATT_BLOCK), lambda b, n: (b, 0, 0))
        bias_spec = pl.BlockSpec(
            (1, BAND, KV_GROUP // 2 * ATT_BLOCK),
            lambda b, n: (jnp.where(n == 0, 0, jnp.where(n == nb - 1, 2, 1)), 0, 0))
        in_specs = [smem, q_spec, kc_spec, vc_spec, bias_spec, kl_spec, vl_spec]
        args = (sink, qt, k_ext, vt_ext, _band_bias(), k_ext, vt_ext)
        first_blk = 0
        aliases = {}
    else:
        nb = ctx_sub
        first_blk = T_LAT // ATT_BLOCK
        q_spec = pl.BlockSpec((D, ATT_BLOCK), lambda b, n: (0, first_blk + b * nb + n))
        in_specs = [smem, q_spec, kc_spec, vc_spec, pl.BlockSpec(memory_space=pl.ANY)]
        args = (sink, qt, k_ext, vt_ext, o_lat)
        aliases = {len(args) - 1: 0}
    return pl.pallas_call(
        functools.partial(_attn_kernel, layer=j, with_band=with_band),
        grid=(BATCH, nb),
        in_specs=in_specs,
        out_specs=pl.BlockSpec((ATT_BLOCK, D), lambda b, n: (first_blk + b * nb + n, 0)),
        out_shape=jax.ShapeDtypeStruct((n_rows, D), bf16),
        input_output_aliases=aliases,
        compiler_params=_params(2),
        name="attn_band" if with_band else "attn_ctx",
    )(*args)


def _proj_kernel(x_ref, o_ref_in, mod_ref, ng_ref, w_ref, out_ref):
    y = _dot(o_ref_in[...], w_ref[0])
    out_ref[...] = x_ref[...] + mod_ref[0, 0][2:3] * _rms(y, ng_ref[0][1:2])


def _proj_layer(x_all, o_all, mod, norm_g, w_o, i, j, n_tok):
    tm = TM_MIX
    row = _mod_row(tm)
    return pl.pallas_call(
        _proj_kernel,
        grid=(n_tok // tm,),
        in_specs=[
            pl.BlockSpec((tm, D), lambda t: (t, 0)),
            pl.BlockSpec((tm, D), lambda t: (t, 0)),
            pl.BlockSpec((1, 1, N_MOD, D), lambda t: (i, row(t), 0, 0)),
            pl.BlockSpec((1, 4, D), lambda t: (i, 0, 0)),
            _resident((1, D, D), lambda t: (j, 0, 0)),
        ],
        out_specs=pl.BlockSpec((tm, D), lambda t: (t, 0)),
        out_shape=jax.ShapeDtypeStruct((n_tok, D), f32),
        compiler_params=_params(1),
        name="attn_proj",
    )(x_all, o_all, mod, norm_g, w_o)


def _row_loop(n_rows, body):
    def step(r, carry):
        body(pl.ds(pl.multiple_of(r * ROW_CHUNK, ROW_CHUNK), ROW_CHUNK))
        return carry
    lax.fori_loop(0, n_rows // ROW_CHUNK, step, 0, unroll=ROW_UNROLL)


def _ffn_kernel(x_ref, mod_ref, ng_ref, wg_ref, wu_ref, wd_ref, o_ref, h_scr, acc_scr):
    hstep = pl.program_id(1)
    tm = x_ref.shape[0]

    @pl.when(hstep == 0)
    def _():
        mod = mod_ref[0, 0]
        gain = ng_ref[0][2:3] * (1.0 + mod[4:5])
        shift = mod[3:4]

        def body(rows):
            xr = x_ref[rows, :]
            inv = lax.rsqrt(jnp.mean(xr * xr, axis=-1, keepdims=True) + EPS)
            h_scr[rows, :] = (xr * inv * gain + shift).astype(bf16)
            acc_scr[rows, :] = jnp.zeros((ROW_CHUNK, D), f32)
        _row_loop(tm, body)

    h = h_scr[...]
    g = _dot(h, wg_ref[0].astype(bf16))
    u = _dot(h, wu_ref[0].astype(bf16))
    a = (g * jax.nn.sigmoid(g) * u).astype(bf16)
    wd = wd_ref[0].astype(bf16)
    for c in range(D // FFN_DOWN_COLS):
        cols = slice(c * FFN_DOWN_COLS, (c + 1) * FFN_DOWN_COLS)
        acc_scr[:, cols] += _dot(a, wd[:, cols])

    @pl.when(hstep == pl.num_programs(1) - 1)
    def _():
        gain = mod_ref[0, 0][5:6] * ng_ref[0][3:4]

        def body(rows):
            y = acc_scr[rows, :]
            inv = lax.rsqrt(jnp.mean(y * y, axis=-1, keepdims=True) + EPS)
            o_ref[rows, :] = x_ref[rows, :] + y * inv * gain
        _row_loop(tm, body)


def _ffn_layer(x_all, mod, norm_g, w_gu, w_down, i, n_tok):
    tm, th = TM_FFN, TH_FFN
    row = _mod_row(tm)
    nh = FFN_HIDDEN // th
    return pl.pallas_call(
        _ffn_kernel,
        grid=(n_tok // tm, nh),
        in_specs=[
            pl.BlockSpec((tm, D), lambda t, h: (t, 0)),
            pl.BlockSpec((1, 1, N_MOD, D), lambda t, h: (i, row(t), 0, 0)),
            pl.BlockSpec((1, 4, D), lambda t, h: (i, 0, 0)),
            pl.BlockSpec((1, D, th), lambda t, h: (i, 0, h)),
            pl.BlockSpec((1, D, th), lambda t, h: (i, 0, nh + h)),
            pl.BlockSpec((1, th, D), lambda t, h: (i, h, 0)),
        ],
        out_specs=pl.BlockSpec((tm, D), lambda t, h: (t, 0)),
        out_shape=jax.ShapeDtypeStruct((n_tok, D), f32),
        scratch_shapes=[pltpu.VMEM((tm, D), bf16), pltpu.VMEM((tm, D), f32)],
        compiler_params=_params(2, BIG_VMEM_LIMIT),
        name="swiglu_ffn",
    )(x_all, mod, norm_g, w_gu, w_gu, w_down)


def _rope_tables():
    rows = SEQ // GRID_W
    pos = np.stack([np.repeat(np.arange(rows), GRID_W), np.tile(np.arange(GRID_W), rows)], -1)
    n_freq = HEAD_DIM // 4
    inv = jnp.asarray(ROPE_THETA, f32) ** (-jnp.arange(n_freq, dtype=f32) / n_freq)
    ang = jnp.asarray(pos, f32)[:, :, None] * inv
    cos, sin = jnp.cos(ang), jnp.sin(ang)
    zero = jnp.zeros_like(sin)
    cos_h = jnp.concatenate([cos[:, 0], cos[:, 0], cos[:, 1], cos[:, 1]], -1)
    sa_h = jnp.concatenate([zero[:, 0], sin[:, 0], zero[:, 1], sin[:, 1]], -1)
    sb_h = jnp.concatenate([-sin[:, 0], zero[:, 0], -sin[:, 1], zero[:, 1]], -1)
    ident = (jnp.ones((TM_MIX, HEAD_DIM), f32), jnp.zeros((TM_MIX, HEAD_DIM), f32),
             jnp.zeros((TM_MIX, HEAD_DIM), f32))
    return tuple(jnp.tile(jnp.concatenate([t, e], 0), (1, LANES // HEAD_DIM))
                 for t, e in zip((cos_h, sa_h, sb_h), ident))


def kernel(x, c, ctx, c_ctx, w_mod, b_mod, norm_g, a_w_in, a_v_norm_g, a_w_s, a_b_s, a_w_out,
           b_w_qkv, b_sink, b_w_o, f_w_gu, f_w_down):
    assert x.shape == (BATCH, SEQ, D) and ctx.shape == (BATCH, CTX_LEN, D)
    xs = (x.reshape(T_LAT, D), ctx.reshape(T_CTX, D))
    c_all = jnp.concatenate(
        [c, c_ctx[None], jnp.zeros((MOD_ROWS - BATCH - 1, D), f32)], axis=0)
    mod = _modulation(c_all, w_mod, b_mod).reshape(DEPTH, MOD_ROWS, N_MOD, D)

    w_in, w_out = a_w_in.astype(bf16), a_w_out.astype(bf16)
    w_s = a_w_s.astype(bf16)
    b_full = jnp.repeat(jnp.swapaxes(a_b_s, 1, 2), GROUP_DIM, axis=2)
    v_g = a_v_norm_g.reshape(-1, 1, D)
    w_qkv, w_o = b_w_qkv.astype(bf16), b_w_o.astype(bf16)
    w_gu, w_down = f_w_gu, f_w_down
    rope = _rope_tables()

    for i in range(DEPTH):
        last = i == DEPTH - 1
        j = i // 2
        n_tok = T_LAT if last else T_ALL
        if i % 2 == 0:
            x_all = _gmlp_layer(xs if i == 0 else (x_all,), mod, norm_g, w_in, v_g, w_s, b_full, w_out,
                                i, j, n_tok)
        else:
            qt, k_ext, vt_ext = _qkv_layer(x_all, mod, norm_g, w_qkv, rope, i, j)
            o = _attention(b_sink, qt, k_ext, vt_ext, j, n_tok)
            if not last:
                o = _attention(b_sink, qt, k_ext, vt_ext, j, n_tok, o_lat=o)
            x_all = _proj_layer(x_all, o, mod, norm_g, w_o, i, j, n_tok)
        x_all = _ffn_layer(x_all, mod, norm_g, w_gu, w_down, i, n_tok)
    return x_all.reshape(BATCH, SEQ, D)
```

```python
import functools

import jax
import jax.numpy as jnp
import numpy as np
from jax import lax
from jax.experimental import pallas as pl
from jax.experimental.pallas import tpu as pltpu

D = 2048
BATCH = 4
SEQ = 2048
DEPTH = 4
GRID_W = 64
CTX_LEN = 256
CHUNK = 128
GROUP_DIM = 128
GROUPS = D // GROUP_DIM
HEAD_DIM = 64
N_Q_HEADS = D // HEAD_DIM
KV_GROUP = 8
N_KV_HEADS = N_Q_HEADS // KV_GROUP
WINDOW = 128
ATT_BLOCK = 128
BAND = 3 * ATT_BLOCK
ROPE_THETA = 10000.0
FFN_HIDDEN = 5632
N_MOD = 6
EPS = 1e-6
NEG_INF = -1e30

T_LAT = BATCH * SEQ
T_CTX = BATCH * CTX_LEN
T_ALL = T_LAT + T_CTX
MOD_ROWS = 8
LANES = 128
LOG2E = float(np.log2(np.e))
PAIR = 2 * HEAD_DIM
KV_EXT = N_KV_HEADS * 2 * PAIR

TM_MIX = 256
TM_GMLP = 256
GMLP_SUB = 128
TM_FFN = 1024
TH_FFN = 256
FFN_DOWN_COLS = 512
ROW_CHUNK = 16
ROW_UNROLL = 16
ATTN_LOOKAHEAD = 2
TN_MOD = 1024
VMEM_LIMIT = 56 * 1024 * 1024
BIG_VMEM_LIMIT = 63 * 1024 * 1024

bf16 = jnp.bfloat16
f32 = jnp.float32


def _rms(x, g):
    return x * lax.rsqrt(jnp.mean(x * x, axis=-1, keepdims=True) + EPS) * g


def _dot(a, b):
    return jnp.dot(a, b, preferred_element_type=f32)


def _dot_nt(a, b):
    return lax.dot_general(a, b, (((1,), (1,)), ((), ())), preferred_element_type=f32)


def _mod_row(tm):
    return lambda t: jnp.minimum((t * tm) // SEQ, BATCH)


def _params(n_axes, vmem=VMEM_LIMIT):
    return pltpu.CompilerParams(
        dimension_semantics=("arbitrary",) * n_axes, vmem_limit_bytes=vmem)


def _resident(shape, index_map):
    return pl.BlockSpec(shape, index_map, pipeline_mode=pl.Buffered(1))


def _mod_kernel(c_ref, w_ref, b_ref, o_ref):
    cc = c_ref[...]
    s = (cc * jax.nn.sigmoid(cc)).astype(bf16)
    o_ref[0] = _dot(s, w_ref[0].astype(bf16)) + b_ref[0]


def _modulation(c_all, w_mod, b_mod):
    n = N_MOD * D
    return pl.pallas_call(
        _mod_kernel,
        grid=(DEPTH, n // TN_MOD),
        in_specs=[
            pl.BlockSpec((MOD_ROWS, D), lambda i, j: (0, 0)),
            pl.BlockSpec((1, D, TN_MOD), lambda i, j: (i, 0, j)),
            pl.BlockSpec((1, 1, TN_MOD), lambda i, j: (i, 0, j)),
        ],
        out_specs=pl.BlockSpec((1, MOD_ROWS, TN_MOD), lambda i, j: (i, 0, j)),
        out_shape=jax.ShapeDtypeStruct((DEPTH, MOD_ROWS, n), f32),
        compiler_params=_params(2),
        name="modulation",
    )(c_all, w_mod, b_mod.reshape(DEPTH, 1, n))


def _gmlp_kernel(x_ref, *rest, split_input):
    if split_input:
        c_ref, *rest = rest
    mod_ref, ng_ref, win_ref, vg_ref, ws_ref, bs_ref, wout_ref, o_ref = rest
    mod = mod_ref[0, 0]
    ng = ng_ref[0]
    n_sub = TM_GMLP // GMLP_SUB
    n_chunks = GMLP_SUB // CHUNK
    xs, zs = [], []
    for r in range(n_sub):
        rows = slice(r * GMLP_SUB, (r + 1) * GMLP_SUB)
        if split_input:
            x = jnp.where(pl.program_id(0) < T_LAT // TM_GMLP, x_ref[rows, :], c_ref[rows, :])
        else:
            x = x_ref[rows, :]
        h = _rms(x, ng[0:1]) * (1.0 + mod[1:2]) + mod[0:1]
        xs.append(x)
        zs.append(_dot(h.astype(bf16), win_ref[0]))
    for r in range(n_sub):
        z = zs[r]
        z = 0.5 * z * (1.0 + lax.erf(z * np.float32(np.sqrt(0.5))))
        u = z[:, :D]
        v = z[:, D:]
        mu = jnp.mean(v, axis=-1, keepdims=True)
        vc = v - mu
        var = jnp.mean(vc * vc, axis=-1, keepdims=True)
        vn = (vc * lax.rsqrt(var + EPS) * vg_ref[0]).astype(bf16)
        cols = []
        for g in range(GROUPS):
            c0 = g * GROUP_DIM
            vg = jnp.concatenate(
                [vn[ck * CHUNK:(ck + 1) * CHUNK, c0:c0 + GROUP_DIM] for ck in range(n_chunks)], axis=1)
            sg = _dot(ws_ref[0, g], vg)
            cols.append(jnp.concatenate(
                [sg[:, ck * GROUP_DIM:(ck + 1) * GROUP_DIM] for ck in range(n_chunks)], axis=0))
        s = jnp.concatenate(cols, axis=1) + jnp.concatenate([bs_ref[0]] * n_chunks, axis=0)
        y = _dot((u * s).astype(bf16), wout_ref[0])
        o_ref[r * GMLP_SUB:(r + 1) * GMLP_SUB, :] = xs[r] + mod[2:3] * _rms(y, ng[1:2])


def _gmlp_layer(xs, mod, norm_g, w_in, v_g, w_s, b_full, w_out, i, j, n_tok):
    tm = TM_GMLP
    row = _mod_row(tm)
    n_lat = T_LAT // tm
    if len(xs) == 2:
        x_specs = [pl.BlockSpec((tm, D), lambda t: (jnp.minimum(t, n_lat - 1), 0)),
                   pl.BlockSpec((tm, D), lambda t: (jnp.maximum(t - n_lat, 0), 0))]
    else:
        x_specs = [pl.BlockSpec((tm, D), lambda t: (t, 0))]
    return pl.pallas_call(
        functools.partial(_gmlp_kernel, split_input=len(xs) == 2),
        grid=(n_tok // tm,),
        in_specs=x_specs + [
            pl.BlockSpec((1, 1, N_MOD, D), lambda t: (i, row(t), 0, 0)),
            pl.BlockSpec((1, 4, D), lambda t: (i, 0, 0)),
            _resident((1, D, 2 * D), lambda t: (j, 0, 0)),
            pl.BlockSpec((1, 1, D), lambda t: (j, 0, 0)),
            _resident((1, GROUPS, CHUNK, CHUNK), lambda t: (j, 0, 0, 0)),
            _resident((1, CHUNK, D), lambda t: (j, 0, 0)),
            _resident((1, D, D), lambda t: (j, 0, 0)),
        ],
        out_specs=pl.BlockSpec((tm, D), lambda t: (t, 0)),
        out_shape=jax.ShapeDtypeStruct((n_tok, D), f32),
        compiler_params=_params(1, BIG_VMEM_LIMIT),
        name="gmlp_mixer",
    )(*xs, mod, norm_g, w_in, v_g, w_s, b_full, w_out)


def _rope(xt, cos, sa, sb):
    return xt * cos + pltpu.roll(xt, 16, 1) * sa + pltpu.roll(xt, LANES - 16, 1) * sb


def _qkv_kernel(x_ref, mod_ref, ng_ref, w_ref, cos_ref, sa_ref, sb_ref, qt_ref, k_ref, vt_ref):
    x = x_ref[...]
    mod = mod_ref[0, 0]
    h = _rms(x, ng_ref[0][0:1]) * (1.0 + mod[1:2]) + mod[0:1]
    qkv = _dot(h.astype(bf16), w_ref[0])
    cos, sa, sb = cos_ref[...], sa_ref[...], sb_ref[...]
    scale = np.float32(HEAD_DIM ** -0.5 * LOG2E)
    n_blk = x.shape[0] // ATT_BLOCK
    for p in range(D // LANES):
        c0 = p * LANES
        qt = _rope(qkv[:, c0:c0 + LANES], cos, sa, sb) * scale
        for r in range(n_blk):
            r0 = r * ATT_BLOCK
            qt_ref[c0:c0 + LANES, r0:r0 + ATT_BLOCK] = qt[r0:r0 + ATT_BLOCK].T.astype(bf16)
    lane = lax.broadcasted_iota(jnp.int32, (x.shape[0], LANES), 1)
    lo = lane < HEAD_DIM
    for m in range(N_KV_HEADS // 2):
        kt = _rope(qkv[:, D + m * LANES:D + (m + 1) * LANES], cos, sa, sb)
        vt = qkv[:, D + N_KV_HEADS * HEAD_DIM + m * LANES:D + N_KV_HEADS * HEAD_DIM + (m + 1) * LANES]
        base = 2 * m * 2 * PAIR
        sw = pltpu.roll(kt, HEAD_DIM, 1)
        k_ref[:, base:base + PAIR] = jnp.where(lo, kt, 0.0).astype(bf16)
        k_ref[:, base + PAIR:base + 2 * PAIR] = jnp.where(lo, 0.0, sw).astype(bf16)
        k_ref[:, base + 2 * PAIR:base + 3 * PAIR] = jnp.where(lo, sw, 0.0).astype(bf16)
        k_ref[:, base + 3 * PAIR:base + 4 * PAIR] = jnp.where(lo, 0.0, kt).astype(bf16)
        sw = pltpu.roll(vt, HEAD_DIM, 1)
        one_hi = (lane == HEAD_DIM).astype(f32)
        one_lo = (lane == 0).astype(f32)
        parts = (jnp.where(lo, vt, one_hi), jnp.where(lo, one_lo, sw),
                 jnp.where(lo, sw, one_hi), jnp.where(lo, one_lo, vt))
        for e, part in enumerate(parts):
            for r in range(n_blk):
                r0 = r * ATT_BLOCK
                vt_ref[r, base + e * PAIR:base + (e + 1) * PAIR, :] = part[r0:r0 + ATT_BLOCK].T.astype(bf16)


def _qkv_layer(x_all, mod, norm_g, w_qkv, rope, i, j):
    tm = TM_MIX
    row = _mod_row(tm)
    n_lat_tiles = SEQ // tm
    n_blk = tm // ATT_BLOCK
    rope_row = lambda t: (jnp.where(t < T_LAT // tm, t % n_lat_tiles, n_lat_tiles), 0)
    n_qkv = w_qkv.shape[-1]
    return pl.pallas_call(
        _qkv_kernel,
        grid=(T_ALL // tm,),
        in_specs=[
            pl.BlockSpec((tm, D), lambda t: (t, 0)),
            pl.BlockSpec((1, 1, N_MOD, D), lambda t: (i, row(t), 0, 0)),
            pl.BlockSpec((1, 4, D), lambda t: (i, 0, 0)),
            _resident((1, D, n_qkv), lambda t: (j, 0, 0)),
            pl.BlockSpec((tm, LANES), rope_row),
            pl.BlockSpec((tm, LANES), rope_row),
            pl.BlockSpec((tm, LANES), rope_row),
        ],
        out_specs=[
            pl.BlockSpec((D, tm), lambda t: (0, t)),
            pl.BlockSpec((tm, KV_EXT), lambda t: (t, 0)),
            pl.BlockSpec((n_blk, KV_EXT, ATT_BLOCK), lambda t: (t, 0, 0)),
        ],
        out_shape=[
            jax.ShapeDtypeStruct((D, T_ALL), bf16),
            jax.ShapeDtypeStruct((T_ALL, KV_EXT), bf16),
            jax.ShapeDtypeStruct((T_ALL // ATT_BLOCK, KV_EXT, ATT_BLOCK), bf16),
        ],
        compiler_params=_params(1),
        name="qkv_rope",
    )(x_all, mod, norm_g, w_qkv, *rope)


def _attn_kernel(sink_ref, qt_ref, kc_ref, vct_ref, *rest, layer, with_band):
    n_pairs = KV_GROUP // 2
    n_cols = n_pairs * ATT_BLOCK
    if with_band:
        bias_ref, kl_ref, vlt_ref, o_ref = rest
        blk0 = jnp.clip(pl.program_id(1) - 1, 0, (SEQ - BAND) // ATT_BLOCK)
        s0 = pl.multiple_of(blk0 * ATT_BLOCK, ATT_BLOCK)
    else:
        _, o_ref = rest
    even_rows = lax.broadcasted_iota(jnp.int32, (PAIR, n_cols), 0) < HEAD_DIM

    def scores(kh, par):
        qt = jnp.concatenate(
            [qt_ref[(kh * n_pairs + pp) * PAIR:(kh * n_pairs + pp + 1) * PAIR, :] for pp in range(n_pairs)],
            axis=1)
        c0 = (2 * kh + par) * PAIR
        s_ctx = _dot(kc_ref[:, c0:c0 + PAIR], qt)
        s_band = _dot(kl_ref[pl.ds(s0, BAND), c0:c0 + PAIR], qt) + bias_ref[0] if with_band else None
        return s_ctx, s_band

    def weighted_values(kh, par, s_ctx, s_band):
        c0 = (2 * kh + par) * PAIR
        sink = jnp.concatenate(
            [jnp.full((1, ATT_BLOCK), sink_ref[layer, kh * KV_GROUP + 2 * pp + par] * LOG2E, f32)
             for pp in range(n_pairs)], axis=1)
        m = jnp.maximum(jnp.max(s_ctx, axis=0, keepdims=True), sink)
        if with_band:
            m = jnp.maximum(m, jnp.max(s_band, axis=0, keepdims=True))
        vct = jnp.concatenate(
            [vct_ref[r, c0:c0 + PAIR, :] for r in range(CTX_LEN // ATT_BLOCK)], axis=1)
        o = _dot(vct, jnp.exp2(s_ctx - m).astype(bf16))
        if with_band:
            vlt = jnp.concatenate(
                [vlt_ref[blk0 + r, c0:c0 + PAIR, :] for r in range(BAND // ATT_BLOCK)], axis=1)
            o = o + _dot(vlt, jnp.exp2(s_band - m).astype(bf16))
        den_row = HEAD_DIM if par == 0 else 0
        den = o[den_row:den_row + 1, :] + jnp.exp2(sink - m)
        return o * (1.0 / den)

    order = [(kh, par) for kh in range(N_KV_HEADS) for par in range(2)]
    pending = [scores(*order[it]) for it in range(ATTN_LOOKAHEAD)]
    halves = []
    for it, (kh, par) in enumerate(order):
        if it + ATTN_LOOKAHEAD < len(order):
            pending.append(scores(*order[it + ATTN_LOOKAHEAD]))
        halves.append(weighted_values(kh, par, *pending.pop(0)))
        if par == 1:
            acc = jnp.where(even_rows, halves[0], halves[1])
            halves = []
            for pp in range(n_pairs):
                c0 = (kh * n_pairs + pp) * PAIR
                o_ref[:, c0:c0 + PAIR] = acc[:, pp * ATT_BLOCK:(pp + 1) * ATT_BLOCK].T.astype(bf16)


def _band_bias():
    kj = np.arange(BAND)[:, None]
    qi = np.arange(KV_GROUP // 2 * ATT_BLOCK)[None, :] % ATT_BLOCK
    tabs = [np.where(np.abs(qi + rel - kj) <= WINDOW, 0.0, NEG_INF) for rel in (0, ATT_BLOCK, 2 * ATT_BLOCK)]
    return jnp.asarray(np.stack(tabs), f32)


def _attention(sink, qt, k_ext, vt_ext, j, n_rows, o_lat=None):
    with_band = o_lat is None
    ctx_blk = T_LAT // CTX_LEN
    ctx_sub = CTX_LEN // ATT_BLOCK
    smem = pl.BlockSpec(memory_space=pltpu.SMEM)
    kc_spec = pl.BlockSpec((CTX_LEN, KV_EXT), lambda b, n: (ctx_blk + b, 0))
    vc_spec = pl.BlockSpec((ctx_sub, KV_EXT, ATT_BLOCK), lambda b, n: (ctx_blk + b, 0, 0))
    if with_band:
        nb = SEQ // ATT_BLOCK
        q_spec = pl.BlockSpec((D, ATT_BLOCK), lambda b, n: (0, b * nb + n))
        kl_spec = pl.BlockSpec((SEQ, KV_EXT), lambda b, n: (b, 0))
        vl_spec = pl.BlockSpec((nb, KV_EXT, ATT_BLOCK), lambda b, n: (b, 0, 0))
        bias_spec = pl.BlockSpec(
            (1, BAND, KV_GROUP // 2 * ATT_BLOCK),
            lambda b, n: (jnp.where(n == 0, 0, jnp.where(n == nb - 1, 2, 1)), 0, 0))
        in_specs = [smem, q_spec, kc_spec, vc_spec, bias_spec, kl_spec, vl_spec]
        args = (sink, qt, k_ext, vt_ext, _band_bias(), k_ext, vt_ext)
        first_blk = 0
        aliases = {}
    else:
        nb = ctx_sub
        first_blk = T_LAT // ATT_BLOCK
        q_spec = pl.BlockSpec((D, ATT_BLOCK), lambda b, n: (0, first_blk + b * nb + n))
        in_specs = [smem, q_spec, kc_spec, vc_spec, pl.BlockSpec(memory_space=pl.ANY)]
        args = (sink, qt, k_ext, vt_ext, o_lat)
        aliases = {len(args) - 1: 0}
    return pl.pallas_call(
        functools.partial(_attn_kernel, layer=j, with_band=with_band),
        grid=(BATCH, nb),
        in_specs=in_specs,
        out_specs=pl.BlockSpec((ATT_BLOCK, D), lambda b, n: (first_blk + b * nb + n, 0)),
        out_shape=jax.ShapeDtypeStruct((n_rows, D), bf16),
        input_output_aliases=aliases,
        compiler_params=_params(2),
        name="attn_band" if with_band else "attn_ctx",
    )(*args)


def _proj_kernel(x_ref, o_ref_in, mod_ref, ng_ref, w_ref, out_ref):
    y = _dot(o_ref_in[...], w_ref[0])
    out_ref[...] = x_ref[...] + mod_ref[0, 0][2:3] * _rms(y, ng_ref[0][1:2])


def _proj_layer(x_all, o_all, mod, norm_g, w_o, i, j, n_tok):
    tm = TM_MIX
    row = _mod_row(tm)
    return pl.pallas_call(
        _proj_kernel,
        grid=(n_tok // tm,),
        in_specs=[
            pl.BlockSpec((tm, D), lambda t: (t, 0)),
            pl.BlockSpec((tm, D), lambda t: (t, 0)),
            pl.BlockSpec((1, 1, N_MOD, D), lambda t: (i, row(t), 0, 0)),
            pl.BlockSpec((1, 4, D), lambda t: (i, 0, 0)),
            _resident((1, D, D), lambda t: (j, 0, 0)),
        ],
        out_specs=pl.BlockSpec((tm, D), lambda t: (t, 0)),
        out_shape=jax.ShapeDtypeStruct((n_tok, D), f32),
        compiler_params=_params(1),
        name="attn_proj",
    )(x_all, o_all, mod, norm_g, w_o)


def _row_loop(n_rows, body):
    def step(r, carry):
        body(pl.ds(pl.multiple_of(r * ROW_CHUNK, ROW_CHUNK), ROW_CHUNK))
        return carry
    lax.fori_loop(0, n_rows // ROW_CHUNK, step, 0, unroll=ROW_UNROLL)


def _ffn_kernel(x_ref, mod_ref, ng_ref, wg_ref, wu_ref, wd_ref, o_ref, h_scr, acc_scr, gu0_scr, gu1_scr, *, nh):
    hstep = pl.program_id(1)
    tm = x_ref.shape[0]
    th = wg_ref.shape[-1]

    def gate_or_up(w_ref):
        return _dot(h_scr[...], w_ref[0].astype(bf16))

    def down(gu_scr):
        g = gu_scr[:, :th]
        a = (g * jax.nn.sigmoid(g) * gu_scr[:, th:]).astype(bf16)
        wd = wd_ref[0].astype(bf16)
        for c in range(D // FFN_DOWN_COLS):
            cols = slice(c * FFN_DOWN_COLS, (c + 1) * FFN_DOWN_COLS)
            acc_scr[:, cols] += _dot(a, wd[:, cols])

    def steady(gu_new, gu_old):
        gu_new[:, :th] = gate_or_up(wg_ref)
        down(gu_old)
        gu_new[:, th:] = gate_or_up(wu_ref)

    @pl.when(hstep == 0)
    def _():
        mod = mod_ref[0, 0]
        gain = ng_ref[0][2:3] * (1.0 + mod[4:5])
        shift = mod[3:4]

        def body(rows):
            xr = x_ref[rows, :]
            inv = lax.rsqrt(jnp.mean(xr * xr, axis=-1, keepdims=True) + EPS)
            h_scr[rows, :] = (xr * inv * gain + shift).astype(bf16)
            acc_scr[rows, :] = jnp.zeros((ROW_CHUNK, D), f32)
        _row_loop(tm, body)
        gu0_scr[:, :th] = gate_or_up(wg_ref)
        gu0_scr[:, th:] = gate_or_up(wu_ref)

    inner = (hstep > 0) & (hstep < nh)

    @pl.when(inner & (hstep % 2 == 1))
    def _():
        steady(gu1_scr, gu0_scr)

    @pl.when(inner & (hstep % 2 == 0))
    def _():
        steady(gu0_scr, gu1_scr)

    @pl.when(hstep == nh)
    def _():
        down(gu1_scr if nh % 2 == 0 else gu0_scr)
        gain = mod_ref[0, 0][5:6] * ng_ref[0][3:4]

        def body(rows):
            y = acc_scr[rows, :]
            inv = lax.rsqrt(jnp.mean(y * y, axis=-1, keepdims=True) + EPS)
            o_ref[rows, :] = x_ref[rows, :] + y * inv * gain
        _row_loop(tm, body)


def _ffn_layer(x_all, mod, norm_g, w_gu, w_down, i, n_tok):
    tm, th = TM_FFN, TH_FFN
    row = _mod_row(tm)
    nh = FFN_HIDDEN // th
    up_blk = lambda h: jnp.minimum(h, nh - 1)
    return pl.pallas_call(
        functools.partial(_ffn_kernel, nh=nh),
        grid=(n_tok // tm, nh + 1),
        in_specs=[
            pl.BlockSpec((tm, D), lambda t, h: (t, 0)),
            pl.BlockSpec((1, 1, N_MOD, D), lambda t, h: (i, row(t), 0, 0)),
            pl.BlockSpec((1, 4, D), lambda t, h: (i, 0, 0)),
            pl.BlockSpec((1, D, th), lambda t, h: (i, 0, up_blk(h))),
            pl.BlockSpec((1, D, th), lambda t, h: (i, 0, nh + up_blk(h))),
            pl.BlockSpec((1, th, D), lambda t, h: (i, jnp.maximum(h - 1, 0), 0)),
        ],
        out_specs=pl.BlockSpec((tm, D), lambda t, h: (t, 0)),
        out_shape=jax.ShapeDtypeStruct((n_tok, D), f32),
        scratch_shapes=[pltpu.VMEM((tm, D), bf16), pltpu.VMEM((tm, D), f32),
                        pltpu.VMEM((tm, 2 * th), f32), pltpu.VMEM((tm, 2 * th), f32)],
        compiler_params=_params(2, BIG_VMEM_LIMIT),
        name="swiglu_ffn",
    )(x_all, mod, norm_g, w_gu, w_gu, w_down)


def _rope_tables():
    rows = SEQ // GRID_W
    pos = np.stack([np.repeat(np.arange(rows), GRID_W), np.tile(np.arange(GRID_W), rows)], -1)
    n_freq = HEAD_DIM // 4
    inv = jnp.asarray(ROPE_THETA, f32) ** (-jnp.arange(n_freq, dtype=f32) / n_freq)
    ang = jnp.asarray(pos, f32)[:, :, None] * inv
    cos, sin = jnp.cos(ang), jnp.sin(ang)
    zero = jnp.zeros_like(sin)
    cos_h = jnp.concatenate([cos[:, 0], cos[:, 0], cos[:, 1], cos[:, 1]], -1)
    sa_h = jnp.concatenate([zero[:, 0], sin[:, 0], zero[:, 1], sin[:, 1]], -1)
    sb_h = jnp.concatenate([-sin[:, 0], zero[:, 0], -sin[:, 1], zero[:, 1]], -1)
    ident = (jnp.ones((TM_MIX, HEAD_DIM), f32), jnp.zeros((TM_MIX, HEAD_DIM), f32),
             jnp.zeros((TM_MIX, HEAD_DIM), f32))
    return tuple(jnp.tile(jnp.concatenate([t, e], 0), (1, LANES // HEAD_DIM))
                 for t, e in zip((cos_h, sa_h, sb_h), ident))


def kernel(x, c, ctx, c_ctx, w_mod, b_mod, norm_g, a_w_in, a_v_norm_g, a_w_s, a_b_s, a_w_out,
           b_w_qkv, b_sink, b_w_o, f_w_gu, f_w_down):
    assert x.shape == (BATCH, SEQ, D) and ctx.shape == (BATCH, CTX_LEN, D)
    xs = (x.reshape(T_LAT, D), ctx.reshape(T_CTX, D))
    c_all = jnp.concatenate(
        [c, c_ctx[None], jnp.zeros((MOD_ROWS - BATCH - 1, D), f32)], axis=0)
    mod = _modulation(c_all, w_mod, b_mod).reshape(DEPTH, MOD_ROWS, N_MOD, D)

    w_in, w_out = a_w_in.astype(bf16), a_w_out.astype(bf16)
    w_s = a_w_s.astype(bf16)
    b_full = jnp.repeat(jnp.swapaxes(a_b_s, 1, 2), GROUP_DIM, axis=2)
    v_g = a_v_norm_g.reshape(-1, 1, D)
    w_qkv, w_o = b_w_qkv.astype(bf16), b_w_o.astype(bf16)
    w_gu, w_down = f_w_gu, f_w_down
    rope = _rope_tables()

    for i in range(DEPTH):
        last = i == DEPTH - 1
        j = i // 2
        n_tok = T_LAT if last else T_ALL
        if i % 2 == 0:
            x_all = _gmlp_layer(xs if i == 0 else (x_all,), mod, norm_g, w_in, v_g, w_s, b_full, w_out,
                                i, j, n_tok)
        else:
            qt, k_ext, vt_ext = _qkv_layer(x_all, mod, norm_g, w_qkv, rope, i, j)
            o = _attention(b_sink, qt, k_ext, vt_ext, j, n_tok)
            if not last:
                o = _attention(b_sink, qt, k_ext, vt_ext, j, n_tok, o_lat=o)
            x_all = _proj_layer(x_all, o, mod, norm_g, w_o, i, j, n_tok)
        x_all = _ffn_layer(x_all, mod, norm_g, w_gu, w_down, i, n_tok)
    return x_all.reshape(BATCH, SEQ, D)
```

```python
import functools

import jax
import jax.numpy as jnp
import numpy as np
from jax import lax
from jax.experimental import pallas as pl
from jax.experimental.pallas import tpu as pltpu

D = 2048
BATCH = 4
SEQ = 2048
DEPTH = 4
GRID_W = 64
CTX_LEN = 256
CHUNK = 128
GROUP_DIM = 128
GROUPS = D // GROUP_DIM
HEAD_DIM = 64
N_Q_HEADS = D // HEAD_DIM
KV_GROUP = 8
N_KV_HEADS = N_Q_HEADS // KV_GROUP
WINDOW = 128
ATT_BLOCK = 128
BAND = 3 * ATT_BLOCK
ROPE_THETA = 10000.0
FFN_HIDDEN = 5632
N_MOD = 6
EPS = 1e-6
NEG_INF = -1e30

T_LAT = BATCH * SEQ
T_CTX = BATCH * CTX_LEN
T_ALL = T_LAT + T_CTX
MOD_ROWS = 8
LANES = 128
LOG2E = float(np.log2(np.e))
PAIR = 2 * HEAD_DIM
KV_EXT = N_KV_HEADS * 2 * PAIR

TM_MIX = 512
MIX_SUB = 256
TM_GMLP = 256
GMLP_SUB = 128
TM_FFN = 1024
TH_FFN = 256
FFN_DOWN_COLS = 512
ROW_CHUNK = 16
ROW_UNROLL = 16
ATTN_LOOKAHEAD = 2
TN_MOD = 1024
VMEM_LIMIT = 56 * 1024 * 1024
BIG_VMEM_LIMIT = 60 * 1024 * 1024

bf16 = jnp.bfloat16
f32 = jnp.float32


def _rms(x, g):
    return x * lax.rsqrt(jnp.mean(x * x, axis=-1, keepdims=True) + EPS) * g


def _dot(a, b):
    return jnp.dot(a, b, preferred_element_type=f32)


def _dot_nt(a, b):
    return lax.dot_general(a, b, (((1,), (1,)), ((), ())), preferred_element_type=f32)


def _mod_row(tm):
    return lambda t: jnp.minimum((t * tm) // SEQ, BATCH)


def _params(n_axes, vmem=VMEM_LIMIT):
    return pltpu.CompilerParams(
        dimension_semantics=("arbitrary",) * n_axes, vmem_limit_bytes=vmem)


def _resident(shape, index_map):
    return pl.BlockSpec(shape, index_map, pipeline_mode=pl.Buffered(1))


def _mod_kernel(c_ref, w_ref, b_ref, o_ref):
    cc = c_ref[...]
    s = (cc * jax.nn.sigmoid(cc)).astype(bf16)
    o_ref[0] = _dot(s, w_ref[0].astype(bf16)) + b_ref[0]


def _modulation(c_all, w_mod, b_mod):
    n = N_MOD * D
    return pl.pallas_call(
        _mod_kernel,
        grid=(DEPTH, n // TN_MOD),
        in_specs=[
            pl.BlockSpec((MOD_ROWS, D), lambda i, j: (0, 0)),
            pl.BlockSpec((1, D, TN_MOD), lambda i, j: (i, 0, j)),
            pl.BlockSpec((1, 1, TN_MOD), lambda i, j: (i, 0, j)),
        ],
        out_specs=pl.BlockSpec((1, MOD_ROWS, TN_MOD), lambda i, j: (i, 0, j)),
        out_shape=jax.ShapeDtypeStruct((DEPTH, MOD_ROWS, n), f32),
        compiler_params=_params(2),
        name="modulation",
    )(c_all, w_mod, b_mod.reshape(DEPTH, 1, n))


def _gmlp_kernel(x_ref, *rest, split_input):
    if split_input:
        c_ref, *rest = rest
    mod_ref, ng_ref, win_ref, vg_ref, ws_ref, bs_ref, wout_ref, o_ref = rest
    mod = mod_ref[0, 0]
    ng = ng_ref[0]
    n_sub = TM_GMLP // GMLP_SUB
    n_chunks = GMLP_SUB // CHUNK
    xs, zs = [], []
    for r in range(n_sub):
        rows = slice(r * GMLP_SUB, (r + 1) * GMLP_SUB)
        if split_input:
            x = jnp.where(pl.program_id(0) < T_LAT // TM_GMLP, x_ref[rows, :], c_ref[rows, :])
        else:
            x = x_ref[rows, :]
        h = _rms(x, ng[0:1]) * (1.0 + mod[1:2]) + mod[0:1]
        xs.append(x)
        zs.append(_dot(h.astype(bf16), win_ref[0]))
    for r in range(n_sub):
        z = zs[r]
        z = 0.5 * z * (1.0 + lax.erf(z * np.float32(np.sqrt(0.5))))
        u = z[:, :D]
        v = z[:, D:]
        mu = jnp.mean(v, axis=-1, keepdims=True)
        vc = v - mu
        var = jnp.mean(vc * vc, axis=-1, keepdims=True)
        vn = (vc * lax.rsqrt(var + EPS) * vg_ref[0]).astype(bf16)
        cols = []
        for g in range(GROUPS):
            c0 = g * GROUP_DIM
            vg = jnp.concatenate(
                [vn[ck * CHUNK:(ck + 1) * CHUNK, c0:c0 + GROUP_DIM] for ck in range(n_chunks)], axis=1)
            sg = _dot(ws_ref[0, g], vg)
            cols.append(jnp.concatenate(
                [sg[:, ck * GROUP_DIM:(ck + 1) * GROUP_DIM] for ck in range(n_chunks)], axis=0))
        s = jnp.concatenate(cols, axis=1) + jnp.concatenate([bs_ref[0]] * n_chunks, axis=0)
        y = _dot((u * s).astype(bf16), wout_ref[0])
        o_ref[r * GMLP_SUB:(r + 1) * GMLP_SUB, :] = xs[r] + mod[2:3] * _rms(y, ng[1:2])


def _gmlp_layer(xs, mod, norm_g, w_in, v_g, w_s, b_full, w_out, i, j, n_tok):
    tm = TM_GMLP
    row = _mod_row(tm)
    n_lat = T_LAT // tm
    if len(xs) == 2:
        x_specs = [pl.BlockSpec((tm, D), lambda t: (jnp.minimum(t, n_lat - 1), 0)),
                   pl.BlockSpec((tm, D), lambda t: (jnp.maximum(t - n_lat, 0), 0))]
    else:
        x_specs = [pl.BlockSpec((tm, D), lambda t: (t, 0))]
    return pl.pallas_call(
        functools.partial(_gmlp_kernel, split_input=len(xs) == 2),
        grid=(n_tok // tm,),
        in_specs=x_specs + [
            pl.BlockSpec((1, 1, N_MOD, D), lambda t: (i, row(t), 0, 0)),
            pl.BlockSpec((1, 4, D), lambda t: (i, 0, 0)),
            _resident((1, D, 2 * D), lambda t: (j, 0, 0)),
            pl.BlockSpec((1, 1, D), lambda t: (j, 0, 0)),
            _resident((1, GROUPS, CHUNK, CHUNK), lambda t: (j, 0, 0, 0)),
            _resident((1, CHUNK, D), lambda t: (j, 0, 0)),
            _resident((1, D, D), lambda t: (j, 0, 0)),
        ],
        out_specs=pl.BlockSpec((tm, D), lambda t: (t, 0)),
        out_shape=jax.ShapeDtypeStruct((n_tok, D), f32),
        compiler_params=_params(1, BIG_VMEM_LIMIT),
        name="gmlp_mixer",
    )(*xs, mod, norm_g, w_in, v_g, w_s, b_full, w_out)


def _rope(xt, cos, sa, sb):
    return xt * cos + pltpu.roll(xt, 16, 1) * sa + pltpu.roll(xt, LANES - 16, 1) * sb


def _qkv_kernel(x_ref, mod_ref, ng_ref, w_ref, cos_ref, sa_ref, sb_ref, qt_ref, k_ref, vt_ref):
    mod = mod_ref[0, 0]
    n_sub = x_ref.shape[0] // MIX_SUB
    qkvs = []
    for s in range(n_sub):
        x = x_ref[s * MIX_SUB:(s + 1) * MIX_SUB, :]
        h = _rms(x, ng_ref[0][0:1]) * (1.0 + mod[1:2]) + mod[0:1]
        qkvs.append(_dot(h.astype(bf16), w_ref[0]))
    scale = np.float32(HEAD_DIM ** -0.5 * LOG2E)
    n_blk = MIX_SUB // ATT_BLOCK
    lane = lax.broadcasted_iota(jnp.int32, (MIX_SUB, LANES), 1)
    lo = lane < HEAD_DIM
    for s in range(n_sub):
        qkv = qkvs[s]
        rows = slice(s * MIX_SUB, (s + 1) * MIX_SUB)
        cos, sa, sb = cos_ref[rows, :], sa_ref[rows, :], sb_ref[rows, :]
        for p in range(D // LANES):
            c0 = p * LANES
            qt = _rope(qkv[:, c0:c0 + LANES], cos, sa, sb) * scale
            for r in range(n_blk):
                r0 = r * ATT_BLOCK
                qt_ref[c0:c0 + LANES, s * MIX_SUB + r0:s * MIX_SUB + r0 + ATT_BLOCK] = (
                    qt[r0:r0 + ATT_BLOCK].T.astype(bf16))
        for m in range(N_KV_HEADS // 2):
            kt = _rope(qkv[:, D + m * LANES:D + (m + 1) * LANES], cos, sa, sb)
            vt = qkv[:, D + N_KV_HEADS * HEAD_DIM + m * LANES:D + N_KV_HEADS * HEAD_DIM + (m + 1) * LANES]
            base = 2 * m * 2 * PAIR
            sw = pltpu.roll(kt, HEAD_DIM, 1)
            k_ref[rows, base:base + PAIR] = jnp.where(lo, kt, 0.0).astype(bf16)
            k_ref[rows, base + PAIR:base + 2 * PAIR] = jnp.where(lo, 0.0, sw).astype(bf16)
            k_ref[rows, base + 2 * PAIR:base + 3 * PAIR] = jnp.where(lo, sw, 0.0).astype(bf16)
            k_ref[rows, base + 3 * PAIR:base + 4 * PAIR] = jnp.where(lo, 0.0, kt).astype(bf16)
            sw = pltpu.roll(vt, HEAD_DIM, 1)
            one_hi = (lane == HEAD_DIM).astype(f32)
            one_lo = (lane == 0).astype(f32)
            parts = (jnp.where(lo, vt, one_hi), jnp.where(lo, one_lo, sw),
                     jnp.where(lo, sw, one_hi), jnp.where(lo, one_lo, vt))
            for e, part in enumerate(parts):
                for r in range(n_blk):
                    r0 = r * ATT_BLOCK
                    vt_ref[s * n_blk + r, base + e * PAIR:base + (e + 1) * PAIR, :] = (
                        part[r0:r0 + ATT_BLOCK].T.astype(bf16))


def _qkv_layer(x_all, mod, norm_g, w_qkv, rope, i, j):
    tm = TM_MIX
    row = _mod_row(tm)
    n_lat_tiles = SEQ // tm
    n_blk = tm // ATT_BLOCK
    rope_row = lambda t: (jnp.where(t < T_LAT // tm, t % n_lat_tiles, n_lat_tiles), 0)
    n_qkv = w_qkv.shape[-1]
    return pl.pallas_call(
        _qkv_kernel,
        grid=(T_ALL // tm,),
        in_specs=[
            pl.BlockSpec((tm, D), lambda t: (t, 0)),
            pl.BlockSpec((1, 1, N_MOD, D), lambda t: (i, row(t), 0, 0)),
            pl.BlockSpec((1, 4, D), lambda t: (i, 0, 0)),
            _resident((1, D, n_qkv), lambda t: (j, 0, 0)),
            pl.BlockSpec((tm, LANES), rope_row),
            pl.BlockSpec((tm, LANES), rope_row),
            pl.BlockSpec((tm, LANES), rope_row),
        ],
        out_specs=[
            pl.BlockSpec((D, tm), lambda t: (0, t)),
            pl.BlockSpec((tm, KV_EXT), lambda t: (t, 0)),
            pl.BlockSpec((n_blk, KV_EXT, ATT_BLOCK), lambda t: (t, 0, 0)),
        ],
        out_shape=[
            jax.ShapeDtypeStruct((D, T_ALL), bf16),
            jax.ShapeDtypeStruct((T_ALL, KV_EXT), bf16),
            jax.ShapeDtypeStruct((T_ALL // ATT_BLOCK, KV_EXT, ATT_BLOCK), bf16),
        ],
        compiler_params=_params(1),
        name="qkv_rope",
    )(x_all, mod, norm_g, w_qkv, *rope)


def _attn_kernel(sink_ref, qt_ref, kc_ref, vct_ref, *rest, layer, with_band):
    n_pairs = KV_GROUP // 2
    n_cols = n_pairs * ATT_BLOCK
    if with_band:
        bias_ref, kl_ref, vlt_ref, o_ref = rest
        blk0 = jnp.clip(pl.program_id(1) - 1, 0, (SEQ - BAND) // ATT_BLOCK)
        s0 = pl.multiple_of(blk0 * ATT_BLOCK, ATT_BLOCK)
    else:
        _, o_ref = rest
    even_rows = lax.broadcasted_iota(jnp.int32, (PAIR, n_cols), 0) < HEAD_DIM

    def scores(kh, par):
        qt = jnp.concatenate(
            [qt_ref[(kh * n_pairs + pp) * PAIR:(kh * n_pairs + pp + 1) * PAIR, :] for pp in range(n_pairs)],
            axis=1)
        c0 = (2 * kh + par) * PAIR
        s_ctx = _dot(kc_ref[:, c0:c0 + PAIR], qt)
        s_band = _dot(kl_ref[pl.ds(s0, BAND), c0:c0 + PAIR], qt) + bias_ref[0] if with_band else None
        return s_ctx, s_band

    def weighted_values(kh, par, s_ctx, s_band):
        c0 = (2 * kh + par) * PAIR
        sink = jnp.concatenate(
            [jnp.full((1, ATT_BLOCK), sink_ref[layer, kh * KV_GROUP + 2 * pp + par] * LOG2E, f32)
             for pp in range(n_pairs)], axis=1)
        m = jnp.maximum(jnp.max(s_ctx, axis=0, keepdims=True), sink)
        if with_band:
            m = jnp.maximum(m, jnp.max(s_band, axis=0, keepdims=True))
        vct = jnp.concatenate(
            [vct_ref[r, c0:c0 + PAIR, :] for r in range(CTX_LEN // ATT_BLOCK)], axis=1)
        o = _dot(vct, jnp.exp2(s_ctx - m).astype(bf16))
        if with_band:
            vlt = jnp.concatenate(
                [vlt_ref[blk0 + r, c0:c0 + PAIR, :] for r in range(BAND // ATT_BLOCK)], axis=1)
            o = o + _dot(vlt, jnp.exp2(s_band - m).astype(bf16))
        den_row = HEAD_DIM if par == 0 else 0
        den = o[den_row:den_row + 1, :] + jnp.exp2(sink - m)
        return o * (1.0 / den)

    order = [(kh, par) for kh in range(N_KV_HEADS) for par in range(2)]
    pending = [scores(*order[it]) for it in range(ATTN_LOOKAHEAD)]
    halves = []
    for it, (kh, par) in enumerate(order):
        if it + ATTN_LOOKAHEAD < len(order):
            pending.append(scores(*order[it + ATTN_LOOKAHEAD]))
        halves.append(weighted_values(kh, par, *pending.pop(0)))
        if par == 1:
            acc = jnp.where(even_rows, halves[0], halves[1])
            halves = []
            for pp in range(n_pairs):
                c0 = (kh * n_pairs + pp) * PAIR
                o_ref[:, c0:c0 + PAIR] = acc[:, pp * ATT_BLOCK:(pp + 1) * ATT_BLOCK].T.astype(bf16)


def _band_bias():
    kj = np.arange(BAND)[:, None]
    qi = np.arange(KV_GROUP // 2 * ATT_BLOCK)[None, :] % ATT_BLOCK
    tabs = [np.where(np.abs(qi + rel - kj) <= WINDOW, 0.0, NEG_INF) for rel in (0, ATT_BLOCK, 2 * ATT_BLOCK)]
    return jnp.asarray(np.stack(tabs), f32)


def _attention(sink, qt, k_ext, vt_ext, j, n_rows, o_lat=None):
    with_band = o_lat is None
    ctx_blk = T_LAT // CTX_LEN
    ctx_sub = CTX_LEN // ATT_BLOCK
    smem = pl.BlockSpec(memory_space=pltpu.SMEM)
    kc_spec = pl.BlockSpec((CTX_LEN, KV_EXT), lambda b, n: (ctx_blk + b, 0))
    vc_spec = pl.BlockSpec((ctx_sub, KV_EXT, ATT_BLOCK), lambda b, n: (ctx_blk + b, 0, 0))
    if with_band:
        nb = SEQ // ATT_BLOCK
        q_spec = pl.BlockSpec((D, ATT_BLOCK), lambda b, n: (0, b * nb + n))
        kl_spec = pl.BlockSpec((SEQ, KV_EXT), lambda b, n: (b, 0))
        vl_spec = pl.BlockSpec((nb, KV_EXT, ATT_BLOCK), lambda b, n: (b, 0, 0))
        bias_spec = pl.BlockSpec(
            (1, BAND, KV_GROUP // 2 * ATT_BLOCK),
            lambda b, n: (jnp.where(n == 0, 0, jnp.where(n == nb - 1, 2, 1)), 0, 0))
        in_specs = [smem, q_spec, kc_spec, vc_spec, bias_spec, kl_spec, vl_spec]
        args = (sink, qt, k_ext, vt_ext, _band_bias(), k_ext, vt_ext)
        first_blk = 0
        aliases = {}
    else:
        nb = ctx_sub
        first_blk = T_LAT // ATT_BLOCK
        q_spec = pl.BlockSpec((D, ATT_BLOCK), lambda b, n: (0, first_blk + b * nb + n))
        in_specs = [smem, q_spec, kc_spec, vc_spec, pl.BlockSpec(memory_space=pl.ANY)]
        args = (sink, qt, k_ext, vt_ext, o_lat)
        aliases = {len(args) - 1: 0}
    return pl.pallas_call(
        functools.partial(_attn_kernel, layer=j, with_band=with_band),
        grid=(BATCH, nb),
        in_specs=in_specs,
        out_specs=pl.BlockSpec((ATT_BLOCK, D), lambda b, n: (first_blk + b * nb + n, 0)),
        out_shape=jax.ShapeDtypeStruct((n_rows, D), bf16),
        input_output_aliases=aliases,
        compiler_params=_params(2),
        name="attn_band" if with_band else "attn_ctx",
    )(*args)


def _proj_kernel(x_ref, o_ref_in, mod_ref, ng_ref, w_ref, out_ref):
    n_sub = x_ref.shape[0] // MIX_SUB
    ys = [_dot(o_ref_in[s * MIX_SUB:(s + 1) * MIX_SUB, :], w_ref[0]) for s in range(n_sub)]
    for s in range(n_sub):
        rows = slice(s * MIX_SUB, (s + 1) * MIX_SUB)
        out_ref[rows, :] = x_ref[rows, :] + mod_ref[0, 0][2:3] * _rms(ys[s], ng_ref[0][1:2])


def _proj_layer(x_all, o_all, mod, norm_g, w_o, i, j, n_tok):
    tm = TM_MIX
    row = _mod_row(tm)
    return pl.pallas_call(
        _proj_kernel,
        grid=(n_tok // tm,),
        in_specs=[
            pl.BlockSpec((tm, D), lambda t: (t, 0)),
            pl.BlockSpec((tm, D), lambda t: (t, 0)),
            pl.BlockSpec((1, 1, N_MOD, D), lambda t: (i, row(t), 0, 0)),
            pl.BlockSpec((1, 4, D), lambda t: (i, 0, 0)),
            _resident((1, D, D), lambda t: (j, 0, 0)),
        ],
        out_specs=pl.BlockSpec((tm, D), lambda t: (t, 0)),
        out_shape=jax.ShapeDtypeStruct((n_tok, D), f32),
        compiler_params=_params(1),
        name="attn_proj",
    )(x_all, o_all, mod, norm_g, w_o)


def _row_loop(n_rows, body):
    def step(r, carry):
        body(pl.ds(pl.multiple_of(r * ROW_CHUNK, ROW_CHUNK), ROW_CHUNK))
        return carry
    lax.fori_loop(0, n_rows // ROW_CHUNK, step, 0, unroll=ROW_UNROLL)


def _ffn_kernel(x_ref, mod_ref, ng_ref, wg_ref, wu_ref, wd_ref, o_ref, h_scr, acc_scr):
    hstep = pl.program_id(1)
    tm = x_ref.shape[0]

    @pl.when(hstep == 0)
    def _():
        mod = mod_ref[0, 0]
        gain = ng_ref[0][2:3] * (1.0 + mod[4:5])
        shift = mod[3:4]

        def body(rows):
            xr = x_ref[rows, :]
            inv = lax.rsqrt(jnp.mean(xr * xr, axis=-1, keepdims=True) + EPS)
            h_scr[rows, :] = (xr * inv * gain + shift).astype(bf16)
            acc_scr[rows, :] = jnp.zeros((ROW_CHUNK, D), f32)
        _row_loop(tm, body)

    h = h_scr[...]
    g = _dot(h, wg_ref[0].astype(bf16))
    u = _dot(h, wu_ref[0].astype(bf16))
    a = (g * jax.nn.sigmoid(g) * u).astype(bf16)
    wd = wd_ref[0].astype(bf16)
    for c in range(D // FFN_DOWN_COLS):
        cols = slice(c * FFN_DOWN_COLS, (c + 1) * FFN_DOWN_COLS)
        acc_scr[:, cols] += _dot(a, wd[:, cols])

    @pl.when(hstep == pl.num_programs(1) - 1)
    def _():
        gain = mod_ref[0, 0][5:6] * ng_ref[0][3:4]

        def body(rows):
            y = acc_scr[rows, :]
            inv = lax.rsqrt(jnp.mean(y * y, axis=-1, keepdims=True) + EPS)
            o_ref[rows, :] = x_ref[rows, :] + y * inv * gain
        _row_loop(tm, body)


def _ffn_layer(x_all, mod, norm_g, w_gu, w_down, i, n_tok):
    tm, th = TM_FFN, TH_FFN
    row = _mod_row(tm)
    nh = FFN_HIDDEN // th
    return pl.pallas_call(
        _ffn_kernel,
        grid=(n_tok // tm, nh),
        in_specs=[
            pl.BlockSpec((tm, D), lambda t, h: (t, 0)),
            pl.BlockSpec((1, 1, N_MOD, D), lambda t, h: (i, row(t), 0, 0)),
            pl.BlockSpec((1, 4, D), lambda t, h: (i, 0, 0)),
            pl.BlockSpec((1, D, th), lambda t, h: (i, 0, h)),
            pl.BlockSpec((1, D, th), lambda t, h: (i, 0, nh + h)),
            pl.BlockSpec((1, th, D), lambda t, h: (i, h, 0)),
        ],
        out_specs=pl.BlockSpec((tm, D), lambda t, h: (t, 0)),
        out_shape=jax.ShapeDtypeStruct((n_tok, D), f32),
        scratch_shapes=[pltpu.VMEM((tm, D), bf16), pltpu.VMEM((tm, D), f32)],
        compiler_params=_params(2, BIG_VMEM_LIMIT),
        name="swiglu_ffn",
    )(x_all, mod, norm_g, w_gu, w_gu, w_down)


def _rope_tables():
    rows = SEQ // GRID_W
    pos = np.stack([np.repeat(np.arange(rows), GRID_W), np.tile(np.arange(GRID_W), rows)], -1)
    n_freq = HEAD_DIM // 4
    inv = jnp.asarray(ROPE_THETA, f32) ** (-jnp.arange(n_freq, dtype=f32) / n_freq)
    ang = jnp.asarray(pos, f32)[:, :, None] * inv
    cos, sin = jnp.cos(ang), jnp.sin(ang)
    zero = jnp.zeros_like(sin)
    cos_h = jnp.concatenate([cos[:, 0], cos[:, 0], cos[:, 1], cos[:, 1]], -1)
    sa_h = jnp.concatenate([zero[:, 0], sin[:, 0], zero[:, 1], sin[:, 1]], -1)
    sb_h = jnp.concatenate([-sin[:, 0], zero[:, 0], -sin[:, 1], zero[:, 1]], -1)
    ident = (jnp.ones((TM_MIX, HEAD_DIM), f32), jnp.zeros((TM_MIX, HEAD_DIM), f32),
             jnp.zeros((TM_MIX, HEAD_DIM), f32))
    return tuple(jnp.tile(jnp.concatenate([t, e], 0), (1, LANES // HEAD_DIM))
                 for t, e in zip((cos_h, sa_h, sb_h), ident))


def kernel(x, c, ctx, c_ctx, w_mod, b_mod, norm_g, a_w_in, a_v_norm_g, a_w_s, a_b_s, a_w_out,
           b_w_qkv, b_sink, b_w_o, f_w_gu, f_w_down):
    assert x.shape == (BATCH, SEQ, D) and ctx.shape == (BATCH, CTX_LEN, D)
    xs = (x.reshape(T_LAT, D), ctx.reshape(T_CTX, D))
    c_all = jnp.concatenate(
        [c, c_ctx[None], jnp.zeros((MOD_ROWS - BATCH - 1, D), f32)], axis=0)
    mod = _modulation(c_all, w_mod, b_mod).reshape(DEPTH, MOD_ROWS, N_MOD, D)

    w_in, w_out = a_w_in.astype(bf16), a_w_out.astype(bf16)
    w_s = a_w_s.astype(bf16)
    b_full = jnp.repeat(jnp.swapaxes(a_b_s, 1, 2), GROUP_DIM, axis=2)
    v_g = a_v_norm_g.reshape(-1, 1, D)
    w_qkv, w_o = b_w_qkv.astype(bf16), b_w_o.astype(bf16)
    w_gu, w_down = f_w_gu, f_w_down
    rope = _rope_tables()

    for i in range(DEPTH):
        last = i == DEPTH - 1
        j = i // 2
        n_tok = T_LAT if last else T_ALL
        if i % 2 == 0:
            x_all = _gmlp_layer(xs if i == 0 else (x_all,), mod, norm_g, w_in, v_g, w_s, b_full, w_out,
                                i, j, n_tok)
        else:
            qt, k_ext, vt_ext = _qkv_layer(x_all, mod, norm_g, w_qkv, rope, i, j)
            o = _attention(b_sink, qt, k_ext, vt_ext, j, n_tok)
            if not last:
                o = _attention(b_sink, qt, k_ext, vt_ext, j, n_tok, o_lat=o)
            x_all = _proj_layer(x_all, o, mod, norm_g, w_o, i, j, n_tok)
        x_all = _ffn_layer(x_all, mod, norm_g, w_gu, w_down, i, n_tok)
    return x_all.reshape(BATCH, SEQ, D)
```

```python
import functools

import jax
import jax.numpy as jnp
import numpy as np
from jax import lax
from jax.experimental import pallas as pl
from jax.experimental.pallas import tpu as pltpu

D = 2048
BATCH = 4
SEQ = 2048
DEPTH = 4
GRID_W = 64
CTX_LEN = 256
CHUNK = 128
GROUP_DIM = 128
GROUPS = D // GROUP_DIM
HEAD_DIM = 64
N_Q_HEADS = D // HEAD_DIM
KV_GROUP = 8
N_KV_HEADS = N_Q_HEADS // KV_GROUP
WINDOW = 128
ATT_BLOCK = 128
BAND = 3 * ATT_BLOCK
ROPE_THETA = 10000.0
FFN_HIDDEN = 5632
N_MOD = 6
EPS = 1e-6
NEG_INF = -1e30

T_LAT = BATCH * SEQ
T_CTX = BATCH * CTX_LEN
T_ALL = T_LAT + T_CTX
MOD_ROWS = 8
LANES = 128
LOG2E = float(np.log2(np.e))
PAIR = 2 * HEAD_DIM
KV_EXT = N_KV_HEADS * 2 * PAIR

TM_MIX = 512
MIX_SUB = 256
TM_GMLP = 256
GMLP_SUB = 128
TM_FFN = 1024
TH_FFN = 256
CAST_STEPS = 128
FFN_DOWN_COLS = 512
ROW_CHUNK = 16
ROW_UNROLL = 16
ATTN_LOOKAHEAD = 2
TN_MOD = 1024
VMEM_LIMIT = 56 * 1024 * 1024
BIG_VMEM_LIMIT = 60 * 1024 * 1024

bf16 = jnp.bfloat16
f32 = jnp.float32


def _rms(x, g):
    return x * lax.rsqrt(jnp.mean(x * x, axis=-1, keepdims=True) + EPS) * g


def _dot(a, b):
    return jnp.dot(a, b, preferred_element_type=f32)


def _dot_nt(a, b):
    return lax.dot_general(a, b, (((1,), (1,)), ((), ())), preferred_element_type=f32)


def _mod_row(tm):
    return lambda t: jnp.minimum((t * tm) // SEQ, BATCH)


def _params(n_axes, vmem=VMEM_LIMIT):
    return pltpu.CompilerParams(
        dimension_semantics=("arbitrary",) * n_axes, vmem_limit_bytes=vmem)


def _resident(shape, index_map):
    return pl.BlockSpec(shape, index_map, pipeline_mode=pl.Buffered(1))


def _mod_kernel(c_ref, w_ref, b_ref, o_ref):
    cc = c_ref[...]
    s = (cc * jax.nn.sigmoid(cc)).astype(bf16)
    o_ref[0] = _dot(s, w_ref[0].astype(bf16)) + b_ref[0]


def _modulation(c_all, w_mod, b_mod):
    n = N_MOD * D
    return pl.pallas_call(
        _mod_kernel,
        grid=(DEPTH, n // TN_MOD),
        in_specs=[
            pl.BlockSpec((MOD_ROWS, D), lambda i, j: (0, 0)),
            pl.BlockSpec((1, D, TN_MOD), lambda i, j: (i, 0, j)),
            pl.BlockSpec((1, 1, TN_MOD), lambda i, j: (i, 0, j)),
        ],
        out_specs=pl.BlockSpec((1, MOD_ROWS, TN_MOD), lambda i, j: (i, 0, j)),
        out_shape=jax.ShapeDtypeStruct((DEPTH, MOD_ROWS, n), f32),
        compiler_params=_params(2),
        name="modulation",
    )(c_all, w_mod, b_mod.reshape(DEPTH, 1, n))


def _gmlp_kernel(x_ref, *rest, split_input):
    if split_input:
        c_ref, *rest = rest
    mod_ref, ng_ref, win_ref, vg_ref, ws_ref, bs_ref, wout_ref, o_ref = rest
    mod = mod_ref[0, 0]
    ng = ng_ref[0]
    n_sub = TM_GMLP // GMLP_SUB
    n_chunks = GMLP_SUB // CHUNK
    xs, zs = [], []
    for r in range(n_sub):
        rows = slice(r * GMLP_SUB, (r + 1) * GMLP_SUB)
        if split_input:
            x = jnp.where(pl.program_id(0) < T_LAT // TM_GMLP, x_ref[rows, :], c_ref[rows, :])
        else:
            x = x_ref[rows, :]
        h = _rms(x, ng[0:1]) * (1.0 + mod[1:2]) + mod[0:1]
        xs.append(x)
        zs.append(_dot(h.astype(bf16), win_ref[0]))
    for r in range(n_sub):
        z = zs[r]
        z = 0.5 * z * (1.0 + lax.erf(z * np.float32(np.sqrt(0.5))))
        u = z[:, :D]
        v = z[:, D:]
        mu = jnp.mean(v, axis=-1, keepdims=True)
        vc = v - mu
        var = jnp.mean(vc * vc, axis=-1, keepdims=True)
        vn = (vc * lax.rsqrt(var + EPS) * vg_ref[0]).astype(bf16)
        cols = []
        for g in range(GROUPS):
            c0 = g * GROUP_DIM
            vg = jnp.concatenate(
                [vn[ck * CHUNK:(ck + 1) * CHUNK, c0:c0 + GROUP_DIM] for ck in range(n_chunks)], axis=1)
            sg = _dot(ws_ref[0, g], vg)
            cols.append(jnp.concatenate(
                [sg[:, ck * GROUP_DIM:(ck + 1) * GROUP_DIM] for ck in range(n_chunks)], axis=0))
        s = jnp.concatenate(cols, axis=1) + jnp.concatenate([bs_ref[0]] * n_chunks, axis=0)
        y = _dot((u * s).astype(bf16), wout_ref[0])
        o_ref[r * GMLP_SUB:(r + 1) * GMLP_SUB, :] = xs[r] + mod[2:3] * _rms(y, ng[1:2])


def _gmlp_layer(xs, mod, norm_g, w_in, v_g, w_s, b_full, w_out, i, j, n_tok):
    tm = TM_GMLP
    row = _mod_row(tm)
    n_lat = T_LAT // tm
    if len(xs) == 2:
        x_specs = [pl.BlockSpec((tm, D), lambda t: (jnp.minimum(t, n_lat - 1), 0)),
                   pl.BlockSpec((tm, D), lambda t: (jnp.maximum(t - n_lat, 0), 0))]
    else:
        x_specs = [pl.BlockSpec((tm, D), lambda t: (t, 0))]
    return pl.pallas_call(
        functools.partial(_gmlp_kernel, split_input=len(xs) == 2),
        grid=(n_tok // tm,),
        in_specs=x_specs + [
            pl.BlockSpec((1, 1, N_MOD, D), lambda t: (i, row(t), 0, 0)),
            pl.BlockSpec((1, 4, D), lambda t: (i, 0, 0)),
            _resident((1, D, 2 * D), lambda t: (j, 0, 0)),
            pl.BlockSpec((1, 1, D), lambda t: (j, 0, 0)),
            _resident((1, GROUPS, CHUNK, CHUNK), lambda t: (j, 0, 0, 0)),
            _resident((1, CHUNK, D), lambda t: (j, 0, 0)),
            _resident((1, D, D), lambda t: (j, 0, 0)),
        ],
        out_specs=pl.BlockSpec((tm, D), lambda t: (t, 0)),
        out_shape=jax.ShapeDtypeStruct((n_tok, D), f32),
        compiler_params=_params(1, BIG_VMEM_LIMIT),
        name="gmlp_mixer",
    )(*xs, mod, norm_g, w_in, v_g, w_s, b_full, w_out)


def _rope(xt, cos, sa, sb):
    return xt * cos + pltpu.roll(xt, 16, 1) * sa + pltpu.roll(xt, LANES - 16, 1) * sb


def _qkv_kernel(x_ref, mod_ref, ng_ref, w_ref, cos_ref, sa_ref, sb_ref, qt_ref, k_ref, vt_ref):
    mod = mod_ref[0, 0]
    n_sub = x_ref.shape[0] // MIX_SUB
    qkvs = []
    for s in range(n_sub):
        x = x_ref[s * MIX_SUB:(s + 1) * MIX_SUB, :]
        h = _rms(x, ng_ref[0][0:1]) * (1.0 + mod[1:2]) + mod[0:1]
        qkvs.append(_dot(h.astype(bf16), w_ref[0]))
    scale = np.float32(HEAD_DIM ** -0.5 * LOG2E)
    n_blk = MIX_SUB // ATT_BLOCK
    lane = lax.broadcasted_iota(jnp.int32, (MIX_SUB, LANES), 1)
    lo = lane < HEAD_DIM
    for s in range(n_sub):
        qkv = qkvs[s]
        rows = slice(s * MIX_SUB, (s + 1) * MIX_SUB)
        cos, sa, sb = cos_ref[rows, :], sa_ref[rows, :], sb_ref[rows, :]
        for p in range(D // LANES):
            c0 = p * LANES
            qt = _rope(qkv[:, c0:c0 + LANES], cos, sa, sb) * scale
            for r in range(n_blk):
                r0 = r * ATT_BLOCK
                qt_ref[c0:c0 + LANES, s * MIX_SUB + r0:s * MIX_SUB + r0 + ATT_BLOCK] = (
                    qt[r0:r0 + ATT_BLOCK].T.astype(bf16))
        for m in range(N_KV_HEADS // 2):
            kt = _rope(qkv[:, D + m * LANES:D + (m + 1) * LANES], cos, sa, sb)
            vt = qkv[:, D + N_KV_HEADS * HEAD_DIM + m * LANES:D + N_KV_HEADS * HEAD_DIM + (m + 1) * LANES]
            base = 2 * m * 2 * PAIR
            sw = pltpu.roll(kt, HEAD_DIM, 1)
            k_ref[rows, base:base + PAIR] = jnp.where(lo, kt, 0.0).astype(bf16)
            k_ref[rows, base + PAIR:base + 2 * PAIR] = jnp.where(lo, 0.0, sw).astype(bf16)
            k_ref[rows, base + 2 * PAIR:base + 3 * PAIR] = jnp.where(lo, sw, 0.0).astype(bf16)
            k_ref[rows, base + 3 * PAIR:base + 4 * PAIR] = jnp.where(lo, 0.0, kt).astype(bf16)
            sw = pltpu.roll(vt, HEAD_DIM, 1)
            one_hi = (lane == HEAD_DIM).astype(f32)
            one_lo = (lane == 0).astype(f32)
            parts = (jnp.where(lo, vt, one_hi), jnp.where(lo, one_lo, sw),
                     jnp.where(lo, sw, one_hi), jnp.where(lo, one_lo, vt))
            for e, part in enumerate(parts):
                for r in range(n_blk):
                    r0 = r * ATT_BLOCK
                    vt_ref[s * n_blk + r, base + e * PAIR:base + (e + 1) * PAIR, :] = (
                        part[r0:r0 + ATT_BLOCK].T.astype(bf16))


def _qkv_layer(x_all, mod, norm_g, w_qkv, rope, i, j):
    tm = TM_MIX
    row = _mod_row(tm)
    n_lat_tiles = SEQ // tm
    n_blk = tm // ATT_BLOCK
    rope_row = lambda t: (jnp.where(t < T_LAT // tm, t % n_lat_tiles, n_lat_tiles), 0)
    n_qkv = w_qkv.shape[-1]
    return pl.pallas_call(
        _qkv_kernel,
        grid=(T_ALL // tm,),
        in_specs=[
            pl.BlockSpec((tm, D), lambda t: (t, 0)),
            pl.BlockSpec((1, 1, N_MOD, D), lambda t: (i, row(t), 0, 0)),
            pl.BlockSpec((1, 4, D), lambda t: (i, 0, 0)),
            _resident((1, D, n_qkv), lambda t: (j, 0, 0)),
            pl.BlockSpec((tm, LANES), rope_row),
            pl.BlockSpec((tm, LANES), rope_row),
            pl.BlockSpec((tm, LANES), rope_row),
        ],
        out_specs=[
            pl.BlockSpec((D, tm), lambda t: (0, t)),
            pl.BlockSpec((tm, KV_EXT), lambda t: (t, 0)),
            pl.BlockSpec((n_blk, KV_EXT, ATT_BLOCK), lambda t: (t, 0, 0)),
        ],
        out_shape=[
            jax.ShapeDtypeStruct((D, T_ALL), bf16),
            jax.ShapeDtypeStruct((T_ALL, KV_EXT), bf16),
            jax.ShapeDtypeStruct((T_ALL // ATT_BLOCK, KV_EXT, ATT_BLOCK), bf16),
        ],
        compiler_params=_params(1),
        name="qkv_rope",
    )(x_all, mod, norm_g, w_qkv, *rope)


def _attn_kernel(sink_ref, qt_ref, kc_ref, vct_ref, bias_ref, kl_ref, vlt_ref, o_ref, *, layer, ctx_steps):
    n = pl.program_id(1)

    @pl.when(n < SEQ // ATT_BLOCK)
    def _():
        _attn_block(sink_ref, qt_ref, kc_ref, vct_ref, bias_ref, kl_ref, vlt_ref, o_ref, layer, True)

    if ctx_steps:
        @pl.when(n >= SEQ // ATT_BLOCK)
        def _():
            _attn_block(sink_ref, qt_ref, kc_ref, vct_ref, None, None, None, o_ref, layer, False)


def _attn_block(sink_ref, qt_ref, kc_ref, vct_ref, bias_ref, kl_ref, vlt_ref, o_ref, layer, with_band):
    n_pairs = KV_GROUP // 2
    n_cols = n_pairs * ATT_BLOCK
    if with_band:
        blk0 = jnp.clip(pl.program_id(1) - 1, 0, (SEQ - BAND) // ATT_BLOCK)
        s0 = pl.multiple_of(blk0 * ATT_BLOCK, ATT_BLOCK)
    even_rows = lax.broadcasted_iota(jnp.int32, (PAIR, n_cols), 0) < HEAD_DIM

    def scores(kh, par):
        qt = jnp.concatenate(
            [qt_ref[(kh * n_pairs + pp) * PAIR:(kh * n_pairs + pp + 1) * PAIR, :] for pp in range(n_pairs)],
            axis=1)
        c0 = (2 * kh + par) * PAIR
        s_ctx = _dot(kc_ref[:, c0:c0 + PAIR], qt)
        s_band = _dot(kl_ref[pl.ds(s0, BAND), c0:c0 + PAIR], qt) + bias_ref[0] if with_band else None
        return s_ctx, s_band

    def weighted_values(kh, par, s_ctx, s_band):
        c0 = (2 * kh + par) * PAIR
        sink = jnp.concatenate(
            [jnp.full((1, ATT_BLOCK), sink_ref[layer, kh * KV_GROUP + 2 * pp + par] * LOG2E, f32)
             for pp in range(n_pairs)], axis=1)
        m = jnp.maximum(jnp.max(s_ctx, axis=0, keepdims=True), sink)
        if with_band:
            m = jnp.maximum(m, jnp.max(s_band, axis=0, keepdims=True))
        vct = jnp.concatenate(
            [vct_ref[r, c0:c0 + PAIR, :] for r in range(CTX_LEN // ATT_BLOCK)], axis=1)
        o = _dot(vct, jnp.exp2(s_ctx - m).astype(bf16))
        if with_band:
            vlt = jnp.concatenate(
                [vlt_ref[blk0 + r, c0:c0 + PAIR, :] for r in range(BAND // ATT_BLOCK)], axis=1)
            o = o + _dot(vlt, jnp.exp2(s_band - m).astype(bf16))
        den_row = HEAD_DIM if par == 0 else 0
        den = o[den_row:den_row + 1, :] + jnp.exp2(sink - m)
        return o * (1.0 / den)

    order = [(kh, par) for kh in range(N_KV_HEADS) for par in range(2)]
    pending = [scores(*order[it]) for it in range(ATTN_LOOKAHEAD)]
    halves = []
    for it, (kh, par) in enumerate(order):
        if it + ATTN_LOOKAHEAD < len(order):
            pending.append(scores(*order[it + ATTN_LOOKAHEAD]))
        halves.append(weighted_values(kh, par, *pending.pop(0)))
        if par == 1:
            acc = jnp.where(even_rows, halves[0], halves[1])
            halves = []
            for pp in range(n_pairs):
                c0 = (kh * n_pairs + pp) * PAIR
                o_ref[:, c0:c0 + PAIR] = acc[:, pp * ATT_BLOCK:(pp + 1) * ATT_BLOCK].T.astype(bf16)


def _band_bias():
    kj = np.arange(BAND)[:, None]
    qi = np.arange(KV_GROUP // 2 * ATT_BLOCK)[None, :] % ATT_BLOCK
    tabs = [np.where(np.abs(qi + rel - kj) <= WINDOW, 0.0, NEG_INF) for rel in (0, ATT_BLOCK, 2 * ATT_BLOCK)]
    return jnp.asarray(np.stack(tabs), f32)


def _attention(sink, qt, k_ext, vt_ext, j, with_ctx_rows):
    ctx_blk = T_LAT // CTX_LEN
    ctx_sub = CTX_LEN // ATT_BLOCK
    nb = SEQ // ATT_BLOCK
    ctx_steps = ctx_sub if with_ctx_rows else 0
    tok_blk = lambda b, n: jnp.where(n < nb, b * nb + n, T_LAT // ATT_BLOCK + b * ctx_sub + n - nb)
    bias_row = lambda b, n: jnp.where(n == 0, 0, jnp.where(n == nb - 1, 2, 1))
    return pl.pallas_call(
        functools.partial(_attn_kernel, layer=j, ctx_steps=ctx_steps),
        grid=(BATCH, nb + ctx_steps),
        in_specs=[
            pl.BlockSpec(memory_space=pltpu.SMEM),
            pl.BlockSpec((D, ATT_BLOCK), lambda b, n: (0, tok_blk(b, n))),
            pl.BlockSpec((CTX_LEN, KV_EXT), lambda b, n: (ctx_blk + b, 0)),
            pl.BlockSpec((ctx_sub, KV_EXT, ATT_BLOCK), lambda b, n: (ctx_blk + b, 0, 0)),
            pl.BlockSpec((1, BAND, KV_GROUP // 2 * ATT_BLOCK), lambda b, n: (bias_row(b, n), 0, 0)),
            pl.BlockSpec((SEQ, KV_EXT), lambda b, n: (b, 0)),
            pl.BlockSpec((nb, KV_EXT, ATT_BLOCK), lambda b, n: (b, 0, 0)),
        ],
        out_specs=pl.BlockSpec((ATT_BLOCK, D), lambda b, n: (tok_blk(b, n), 0)),
        out_shape=jax.ShapeDtypeStruct((T_ALL if with_ctx_rows else T_LAT, D), bf16),
        compiler_params=_params(2),
        name="attention",
    )(sink, qt, k_ext, vt_ext, _band_bias(), k_ext, vt_ext)


def _proj_kernel(x_ref, o_ref_in, mod_ref, ng_ref, w_ref, out_ref):
    n_sub = x_ref.shape[0] // MIX_SUB
    ys = [_dot(o_ref_in[s * MIX_SUB:(s + 1) * MIX_SUB, :], w_ref[0]) for s in range(n_sub)]
    for s in range(n_sub):
        rows = slice(s * MIX_SUB, (s + 1) * MIX_SUB)
        out_ref[rows, :] = x_ref[rows, :] + mod_ref[0, 0][2:3] * _rms(ys[s], ng_ref[0][1:2])


def _proj_layer(x_all, o_all, mod, norm_g, w_o, i, j, n_tok):
    tm = TM_MIX
    row = _mod_row(tm)
    return pl.pallas_call(
        _proj_kernel,
        grid=(n_tok // tm,),
        in_specs=[
            pl.BlockSpec((tm, D), lambda t: (t, 0)),
            pl.BlockSpec((tm, D), lambda t: (t, 0)),
            pl.BlockSpec((1, 1, N_MOD, D), lambda t: (i, row(t), 0, 0)),
            pl.BlockSpec((1, 4, D), lambda t: (i, 0, 0)),
            _resident((1, D, D), lambda t: (j, 0, 0)),
        ],
        out_specs=pl.BlockSpec((tm, D), lambda t: (t, 0)),
        out_shape=jax.ShapeDtypeStruct((n_tok, D), f32),
        compiler_params=_params(1),
        name="attn_proj",
    )(x_all, o_all, mod, norm_g, w_o)


def _row_loop(n_rows, body):
    def step(r, carry):
        body(pl.ds(pl.multiple_of(r * ROW_CHUNK, ROW_CHUNK), ROW_CHUNK))
        return carry
    lax.fori_loop(0, n_rows // ROW_CHUNK, step, 0, unroll=ROW_UNROLL)


def _ffn_kernel(*refs, n_cast):
    x_ref, mod_ref, ng_ref, wg_ref, wu_ref, wd_ref = refs[:6]
    o_ref = refs[6 + n_cast]
    h_scr, acc_scr = refs[-2:]
    hstep = pl.program_id(1)
    tm = x_ref.shape[0]
    for src_ref, dst_ref in zip(refs[6:6 + n_cast], refs[7 + n_cast:7 + 2 * n_cast]):
        dst_ref[...] = src_ref[...].astype(bf16)

    @pl.when(hstep == 0)
    def _():
        mod = mod_ref[0, 0]
        gain = ng_ref[0][2:3] * (1.0 + mod[4:5])
        shift = mod[3:4]

        def body(rows):
            xr = x_ref[rows, :]
            inv = lax.rsqrt(jnp.mean(xr * xr, axis=-1, keepdims=True) + EPS)
            h_scr[rows, :] = (xr * inv * gain + shift).astype(bf16)
            acc_scr[rows, :] = jnp.zeros((ROW_CHUNK, D), f32)
        _row_loop(tm, body)

    h = h_scr[...]
    g = _dot(h, wg_ref[0].astype(bf16))
    u = _dot(h, wu_ref[0].astype(bf16))
    a = (g * jax.nn.sigmoid(g) * u).astype(bf16)
    wd = wd_ref[0].astype(bf16)
    for c in range(D // FFN_DOWN_COLS):
        cols = slice(c * FFN_DOWN_COLS, (c + 1) * FFN_DOWN_COLS)
        acc_scr[:, cols] += _dot(a, wd[:, cols])

    @pl.when(hstep == pl.num_programs(1) - 1)
    def _():
        gain = mod_ref[0, 0][5:6] * ng_ref[0][3:4]

        def body(rows):
            y = acc_scr[rows, :]
            inv = lax.rsqrt(jnp.mean(y * y, axis=-1, keepdims=True) + EPS)
            o_ref[rows, :] = x_ref[rows, :] + y * inv * gain
        _row_loop(tm, body)


def _ffn_layer(x_all, mod, norm_g, w_gu, w_down, i, n_tok, cast=()):
    tm, th = TM_FFN, TH_FFN
    row = _mod_row(tm)
    nh = FFN_HIDDEN // th
    n_tiles = n_tok // tm
    assert not cast or n_tiles * nh >= CAST_STEPS
    cast_blk = lambda t, h: jnp.minimum(t * nh + h, CAST_STEPS - 1)
    cast_in, cast_out, cast_shapes = [], [], []
    for w, layer in cast:
        _, rows, cols = w.shape
        blk = (1, rows // CAST_STEPS, cols)
        cast_in.append(pl.BlockSpec(blk, lambda t, h, layer=layer: (layer, cast_blk(t, h), 0)))
        cast_out.append(pl.BlockSpec(blk, lambda t, h: (0, cast_blk(t, h), 0)))
        cast_shapes.append(jax.ShapeDtypeStruct((1, rows, cols), bf16))
    outs = pl.pallas_call(
        functools.partial(_ffn_kernel, n_cast=len(cast)),
        grid=(n_tiles, nh),
        in_specs=[
            pl.BlockSpec((tm, D), lambda t, h: (t, 0)),
            pl.BlockSpec((1, 1, N_MOD, D), lambda t, h: (i, row(t), 0, 0)),
            pl.BlockSpec((1, 4, D), lambda t, h: (i, 0, 0)),
            pl.BlockSpec((1, D, th), lambda t, h: (i, 0, h)),
            pl.BlockSpec((1, D, th), lambda t, h: (i, 0, nh + h)),
            pl.BlockSpec((1, th, D), lambda t, h: (i, h, 0)),
        ] + cast_in,
        out_specs=[pl.BlockSpec((tm, D), lambda t, h: (t, 0))] + cast_out,
        out_shape=[jax.ShapeDtypeStruct((n_tok, D), f32)] + cast_shapes,
        scratch_shapes=[pltpu.VMEM((tm, D), bf16), pltpu.VMEM((tm, D), f32)],
        compiler_params=_params(2, BIG_VMEM_LIMIT),
        name="swiglu_ffn",
    )(x_all, mod, norm_g, w_gu, w_gu, w_down, *[w for w, _ in cast])
    return outs[0], tuple(outs[1:])


def _rope_tables():
    rows = SEQ // GRID_W
    pos = np.stack([np.repeat(np.arange(rows), GRID_W), np.tile(np.arange(GRID_W), rows)], -1)
    n_freq = HEAD_DIM // 4
    inv = jnp.asarray(ROPE_THETA, f32) ** (-jnp.arange(n_freq, dtype=f32) / n_freq)
    ang = jnp.asarray(pos, f32)[:, :, None] * inv
    cos, sin = jnp.cos(ang), jnp.sin(ang)
    zero = jnp.zeros_like(sin)
    cos_h = jnp.concatenate([cos[:, 0], cos[:, 0], cos[:, 1], cos[:, 1]], -1)
    sa_h = jnp.concatenate([zero[:, 0], sin[:, 0], zero[:, 1], sin[:, 1]], -1)
    sb_h = jnp.concatenate([-sin[:, 0], zero[:, 0], -sin[:, 1], zero[:, 1]], -1)
    ident = (jnp.ones((TM_MIX, HEAD_DIM), f32), jnp.zeros((TM_MIX, HEAD_DIM), f32),
             jnp.zeros((TM_MIX, HEAD_DIM), f32))
    return tuple(jnp.tile(jnp.concatenate([t, e], 0), (1, LANES // HEAD_DIM))
                 for t, e in zip((cos_h, sa_h, sb_h), ident))


def kernel(x, c, ctx, c_ctx, w_mod, b_mod, norm_g, a_w_in, a_v_norm_g, a_w_s, a_b_s, a_w_out,
           b_w_qkv, b_sink, b_w_o, f_w_gu, f_w_down):
    assert x.shape == (BATCH, SEQ, D) and ctx.shape == (BATCH, CTX_LEN, D)
    xs = (x.reshape(T_LAT, D), ctx.reshape(T_CTX, D))
    c_all = jnp.concatenate(
        [c, c_ctx[None], jnp.zeros((MOD_ROWS - BATCH - 1, D), f32)], axis=0)
    mod = _modulation(c_all, w_mod, b_mod).reshape(DEPTH, MOD_ROWS, N_MOD, D)

    w_s = a_w_s.astype(bf16)
    b_full = jnp.repeat(jnp.swapaxes(a_b_s, 1, 2), GROUP_DIM, axis=2)
    v_g = a_v_norm_g.reshape(-1, 1, D)
    rope = _rope_tables()

    mixer_w = (a_w_in[:1].astype(bf16), a_w_out[:1].astype(bf16))
    for i in range(DEPTH):
        last = i == DEPTH - 1
        j = i // 2
        n_tok = T_LAT if last else T_ALL
        if i % 2 == 0:
            x_all = _gmlp_layer(xs if i == 0 else (x_all,), mod, norm_g, mixer_w[0], v_g[j:j + 1], w_s[j:j + 1],
                                b_full[j:j + 1], mixer_w[1], i, 0, n_tok)
        else:
            qt, k_ext, vt_ext = _qkv_layer(x_all, mod, norm_g, mixer_w[0], rope, i, 0)
            o = _attention(b_sink, qt, k_ext, vt_ext, j, with_ctx_rows=not last)
            x_all = _proj_layer(x_all, o, mod, norm_g, mixer_w[1], i, 0, n_tok)
        if last:
            cast = ()
        elif i % 2 == 0:
            cast = ((b_w_qkv, j), (b_w_o, j))
        else:
            cast = ((a_w_in, j + 1), (a_w_out, j + 1))
        x_all, mixer_w = _ffn_layer(x_all, mod, norm_g, f_w_gu, f_w_down, i, n_tok, cast)
    return x_all.reshape(BATCH, SEQ, D)
```

```python
import functools

import jax
import jax.numpy as jnp
import numpy as np
from jax import lax
from jax.experimental import pallas as pl
from jax.experimental.pallas import tpu as pltpu

D = 2048
BATCH = 4
SEQ = 2048
DEPTH = 4
GRID_W = 64
CTX_LEN = 256
CHUNK = 128
GROUP_DIM = 128
GROUPS = D // GROUP_DIM
HEAD_DIM = 64
N_Q_HEADS = D // HEAD_DIM
KV_GROUP = 8
N_KV_HEADS = N_Q_HEADS // KV_GROUP
WINDOW = 128
ATT_BLOCK = 128
BAND = 3 * ATT_BLOCK
ROPE_THETA = 10000.0
FFN_HIDDEN = 5632
N_MOD = 6
EPS = 1e-6
NEG_INF = -1e30

T_LAT = BATCH * SEQ
T_CTX = BATCH * CTX_LEN
T_ALL = T_LAT + T_CTX
MOD_ROWS = 8
LANES = 128
LOG2E = float(np.log2(np.e))
PAIR = 2 * HEAD_DIM
KV_EXT = N_KV_HEADS * 2 * PAIR

TM_MIX = 512
MIX_SUB = 256
TM_GMLP = 256
GMLP_SUB = 256
TM_FFN = 1024
TH_FFN = 256
CAST_STEPS = 128
FFN_DOWN_COLS = 512
ROW_CHUNK = 16
ROW_UNROLL = 16
ATTN_LOOKAHEAD = 2
TN_MOD = 1024
VMEM_LIMIT = 56 * 1024 * 1024
BIG_VMEM_LIMIT = 60 * 1024 * 1024

bf16 = jnp.bfloat16
f32 = jnp.float32


def _rms(x, g):
    return x * lax.rsqrt(jnp.mean(x * x, axis=-1, keepdims=True) + EPS) * g


def _dot(a, b):
    return jnp.dot(a, b, preferred_element_type=f32)


def _dot_nt(a, b):
    return lax.dot_general(a, b, (((1,), (1,)), ((), ())), preferred_element_type=f32)


def _mod_row(tm):
    return lambda t: jnp.minimum((t * tm) // SEQ, BATCH)


def _params(n_axes, vmem=VMEM_LIMIT):
    return pltpu.CompilerParams(
        dimension_semantics=("arbitrary",) * n_axes, vmem_limit_bytes=vmem)


def _resident(shape, index_map):
    return pl.BlockSpec(shape, index_map, pipeline_mode=pl.Buffered(1))


def _mod_kernel(c_ref, w_ref, b_ref, o_ref):
    cc = c_ref[...]
    s = (cc * jax.nn.sigmoid(cc)).astype(bf16)
    o_ref[0] = _dot(s, w_ref[0].astype(bf16)) + b_ref[0]


def _modulation(c_all, w_mod, b_mod):
    n = N_MOD * D
    return pl.pallas_call(
        _mod_kernel,
        grid=(DEPTH, n // TN_MOD),
        in_specs=[
            pl.BlockSpec((MOD_ROWS, D), lambda i, j: (0, 0)),
            pl.BlockSpec((1, D, TN_MOD), lambda i, j: (i, 0, j)),
            pl.BlockSpec((1, 1, TN_MOD), lambda i, j: (i, 0, j)),
        ],
        out_specs=pl.BlockSpec((1, MOD_ROWS, TN_MOD), lambda i, j: (i, 0, j)),
        out_shape=jax.ShapeDtypeStruct((DEPTH, MOD_ROWS, n), f32),
        compiler_params=_params(2),
        name="modulation",
    )(c_all, w_mod, b_mod.reshape(DEPTH, 1, n))


def _gmlp_kernel(x_ref, *rest, split_input):
    if split_input:
        c_ref, *rest = rest
    mod_ref, ng_ref, win_ref, vg_ref, ws_ref, bs_ref, wout_ref, o_ref = rest
    mod = mod_ref[0, 0]
    ng = ng_ref[0]
    n_sub = TM_GMLP // GMLP_SUB
    n_chunks = GMLP_SUB // CHUNK
    xs, zs = [], []
    for r in range(n_sub):
        rows = slice(r * GMLP_SUB, (r + 1) * GMLP_SUB)
        if split_input:
            x = jnp.where(pl.program_id(0) < T_LAT // TM_GMLP, x_ref[rows, :], c_ref[rows, :])
        else:
            x = x_ref[rows, :]
        h = _rms(x, ng[0:1]) * (1.0 + mod[1:2]) + mod[0:1]
        xs.append(x)
        hb = h.astype(bf16)
        zs.append((_dot(hb, win_ref[0, :, D:]), _dot(hb, win_ref[0, :, :D])))
    gelu = lambda z: 0.5 * z * (1.0 + lax.erf(z * np.float32(np.sqrt(0.5))))
    for r in range(n_sub):
        v = gelu(zs[r][0])
        u = gelu(zs[r][1])
        mu = jnp.mean(v, axis=-1, keepdims=True)
        vc = v - mu
        var = jnp.mean(vc * vc, axis=-1, keepdims=True)
        vn = (vc * lax.rsqrt(var + EPS) * vg_ref[0]).astype(bf16)
        cols = []
        for g in range(GROUPS):
            c0 = g * GROUP_DIM
            vg = jnp.concatenate(
                [vn[ck * CHUNK:(ck + 1) * CHUNK, c0:c0 + GROUP_DIM] for ck in range(n_chunks)], axis=1)
            sg = _dot(ws_ref[0, g], vg)
            cols.append(jnp.concatenate(
                [sg[:, ck * GROUP_DIM:(ck + 1) * GROUP_DIM] for ck in range(n_chunks)], axis=0))
        s = jnp.concatenate(cols, axis=1) + jnp.concatenate([bs_ref[0]] * n_chunks, axis=0)
        y = _dot((u * s).astype(bf16), wout_ref[0])
        o_ref[r * GMLP_SUB:(r + 1) * GMLP_SUB, :] = xs[r] + mod[2:3] * _rms(y, ng[1:2])


def _gmlp_layer(xs, mod, norm_g, w_in, v_g, w_s, b_full, w_out, i, j, n_tok):
    tm = TM_GMLP
    row = _mod_row(tm)
    n_lat = T_LAT // tm
    if len(xs) == 2:
        x_specs = [pl.BlockSpec((tm, D), lambda t: (jnp.minimum(t, n_lat - 1), 0)),
                   pl.BlockSpec((tm, D), lambda t: (jnp.maximum(t - n_lat, 0), 0))]
    else:
        x_specs = [pl.BlockSpec((tm, D), lambda t: (t, 0))]
    return pl.pallas_call(
        functools.partial(_gmlp_kernel, split_input=len(xs) == 2),
        grid=(n_tok // tm,),
        in_specs=x_specs + [
            pl.BlockSpec((1, 1, N_MOD, D), lambda t: (i, row(t), 0, 0)),
            pl.BlockSpec((1, 4, D), lambda t: (i, 0, 0)),
            _resident((1, D, 2 * D), lambda t: (j, 0, 0)),
            pl.BlockSpec((1, 1, D), lambda t: (j, 0, 0)),
            _resident((1, GROUPS, CHUNK, CHUNK), lambda t: (j, 0, 0, 0)),
            _resident((1, CHUNK, D), lambda t: (j, 0, 0)),
            _resident((1, D, D), lambda t: (j, 0, 0)),
        ],
        out_specs=pl.BlockSpec((tm, D), lambda t: (t, 0)),
        out_shape=jax.ShapeDtypeStruct((n_tok, D), f32),
        compiler_params=_params(1, BIG_VMEM_LIMIT),
        name="gmlp_mixer",
    )(*xs, mod, norm_g, w_in, v_g, w_s, b_full, w_out)


def _rope(xt, cos, sa, sb):
    return xt * cos + pltpu.roll(xt, 16, 1) * sa + pltpu.roll(xt, LANES - 16, 1) * sb


def _qkv_kernel(x_ref, mod_ref, ng_ref, w_ref, cos_ref, sa_ref, sb_ref, qt_ref, k_ref, vt_ref):
    mod = mod_ref[0, 0]
    n_sub = x_ref.shape[0] // MIX_SUB
    qkvs = []
    for s in range(n_sub):
        x = x_ref[s * MIX_SUB:(s + 1) * MIX_SUB, :]
        h = _rms(x, ng_ref[0][0:1]) * (1.0 + mod[1:2]) + mod[0:1]
        qkvs.append(_dot(h.astype(bf16), w_ref[0]))
    scale = np.float32(HEAD_DIM ** -0.5 * LOG2E)
    n_blk = MIX_SUB // ATT_BLOCK
    lane = lax.broadcasted_iota(jnp.int32, (MIX_SUB, LANES), 1)
    lo = lane < HEAD_DIM
    for s in range(n_sub):
        qkv = qkvs[s]
        rows = slice(s * MIX_SUB, (s + 1) * MIX_SUB)
        cos, sa, sb = cos_ref[rows, :], sa_ref[rows, :], sb_ref[rows, :]
        for p in range(D // LANES):
            c0 = p * LANES
            qt = _rope(qkv[:, c0:c0 + LANES], cos, sa, sb) * scale
            for r in range(n_blk):
                r0 = r * ATT_BLOCK
                qt_ref[c0:c0 + LANES, s * MIX_SUB + r0:s * MIX_SUB + r0 + ATT_BLOCK] = (
                    qt[r0:r0 + ATT_BLOCK].T.astype(bf16))
        for m in range(N_KV_HEADS // 2):
            kt = _rope(qkv[:, D + m * LANES:D + (m + 1) * LANES], cos, sa, sb)
            vt = qkv[:, D + N_KV_HEADS * HEAD_DIM + m * LANES:D + N_KV_HEADS * HEAD_DIM + (m + 1) * LANES]
            base = 2 * m * 2 * PAIR
            sw = pltpu.roll(kt, HEAD_DIM, 1)
            k_ref[rows, base:base + PAIR] = jnp.where(lo, kt, 0.0).astype(bf16)
            k_ref[rows, base + PAIR:base + 2 * PAIR] = jnp.where(lo, 0.0, sw).astype(bf16)
            k_ref[rows, base + 2 * PAIR:base + 3 * PAIR] = jnp.where(lo, sw, 0.0).astype(bf16)
            k_ref[rows, base + 3 * PAIR:base + 4 * PAIR] = jnp.where(lo, 0.0, kt).astype(bf16)
            sw = pltpu.roll(vt, HEAD_DIM, 1)
            one_hi = (lane == HEAD_DIM).astype(f32)
            one_lo = (lane == 0).astype(f32)
            parts = (jnp.where(lo, vt, one_hi), jnp.where(lo, one_lo, sw),
                     jnp.where(lo, sw, one_hi), jnp.where(lo, one_lo, vt))
            for e, part in enumerate(parts):
                for r in range(n_blk):
                    r0 = r * ATT_BLOCK
                    vt_ref[s * n_blk + r, base + e * PAIR:base + (e + 1) * PAIR, :] = (
                        part[r0:r0 + ATT_BLOCK].T.astype(bf16))


def _qkv_layer(x_all, mod, norm_g, w_qkv, rope, i, j):
    tm = TM_MIX
    row = _mod_row(tm)
    n_lat_tiles = SEQ // tm
    n_blk = tm // ATT_BLOCK
    rope_row = lambda t: (jnp.where(t < T_LAT // tm, t % n_lat_tiles, n_lat_tiles), 0)
    n_qkv = w_qkv.shape[-1]
    return pl.pallas_call(
        _qkv_kernel,
        grid=(T_ALL // tm,),
        in_specs=[
            pl.BlockSpec((tm, D), lambda t: (t, 0)),
            pl.BlockSpec((1, 1, N_MOD, D), lambda t: (i, row(t), 0, 0)),
            pl.BlockSpec((1, 4, D), lambda t: (i, 0, 0)),
            _resident((1, D, n_qkv), lambda t: (j, 0, 0)),
            pl.BlockSpec((tm, LANES), rope_row),
            pl.BlockSpec((tm, LANES), rope_row),
            pl.BlockSpec((tm, LANES), rope_row),
        ],
        out_specs=[
            pl.BlockSpec((D, tm), lambda t: (0, t)),
            pl.BlockSpec((tm, KV_EXT), lambda t: (t, 0)),
            pl.BlockSpec((n_blk, KV_EXT, ATT_BLOCK), lambda t: (t, 0, 0)),
        ],
        out_shape=[
            jax.ShapeDtypeStruct((D, T_ALL), bf16),
            jax.ShapeDtypeStruct((T_ALL, KV_EXT), bf16),
            jax.ShapeDtypeStruct((T_ALL // ATT_BLOCK, KV_EXT, ATT_BLOCK), bf16),
        ],
        compiler_params=_params(1),
        name="qkv_rope",
    )(x_all, mod, norm_g, w_qkv, *rope)


def _attn_kernel(sink_ref, qt_ref, kc_ref, vct_ref, bias_ref, kl_ref, vlt_ref, o_ref, *, layer, ctx_steps):
    n = pl.program_id(1)

    @pl.when(n < SEQ // ATT_BLOCK)
    def _():
        _attn_block(sink_ref, qt_ref, kc_ref, vct_ref, bias_ref, kl_ref, vlt_ref, o_ref, layer, True)

    if ctx_steps:
        @pl.when(n >= SEQ // ATT_BLOCK)
        def _():
            _attn_block(sink_ref, qt_ref, kc_ref, vct_ref, None, None, None, o_ref, layer, False)


def _attn_block(sink_ref, qt_ref, kc_ref, vct_ref, bias_ref, kl_ref, vlt_ref, o_ref, layer, with_band):
    n_pairs = KV_GROUP // 2
    n_cols = n_pairs * ATT_BLOCK
    if with_band:
        blk0 = jnp.clip(pl.program_id(1) - 1, 0, (SEQ - BAND) // ATT_BLOCK)
        s0 = pl.multiple_of(blk0 * ATT_BLOCK, ATT_BLOCK)
    even_rows = lax.broadcasted_iota(jnp.int32, (PAIR, n_cols), 0) < HEAD_DIM

    def scores(kh, par):
        qt = jnp.concatenate(
            [qt_ref[(kh * n_pairs + pp) * PAIR:(kh * n_pairs + pp + 1) * PAIR, :] for pp in range(n_pairs)],
            axis=1)
        c0 = (2 * kh + par) * PAIR
        s_ctx = _dot(kc_ref[:, c0:c0 + PAIR], qt)
        s_band = _dot(kl_ref[pl.ds(s0, BAND), c0:c0 + PAIR], qt) + bias_ref[0] if with_band else None
        return s_ctx, s_band

    def weighted_values(kh, par, s_ctx, s_band):
        c0 = (2 * kh + par) * PAIR
        sink = jnp.concatenate(
            [jnp.full((1, ATT_BLOCK), sink_ref[layer, kh * KV_GROUP + 2 * pp + par] * LOG2E, f32)
             for pp in range(n_pairs)], axis=1)
        m = jnp.maximum(jnp.max(s_ctx, axis=0, keepdims=True), sink)
        if with_band:
            m = jnp.maximum(m, jnp.max(s_band, axis=0, keepdims=True))
        vct = jnp.concatenate(
            [vct_ref[r, c0:c0 + PAIR, :] for r in range(CTX_LEN // ATT_BLOCK)], axis=1)
        o = _dot(vct, jnp.exp2(s_ctx - m).astype(bf16))
        if with_band:
            vlt = jnp.concatenate(
                [vlt_ref[blk0 + r, c0:c0 + PAIR, :] for r in range(BAND // ATT_BLOCK)], axis=1)
            o = o + _dot(vlt, jnp.exp2(s_band - m).astype(bf16))
        den_row = HEAD_DIM if par == 0 else 0
        den = o[den_row:den_row + 1, :] + jnp.exp2(sink - m)
        return o * (1.0 / den)

    order = [(kh, par) for kh in range(N_KV_HEADS) for par in range(2)]
    pending = [scores(*order[it]) for it in range(ATTN_LOOKAHEAD)]
    halves = []
    for it, (kh, par) in enumerate(order):
        if it + ATTN_LOOKAHEAD < len(order):
            pending.append(scores(*order[it + ATTN_LOOKAHEAD]))
        halves.append(weighted_values(kh, par, *pending.pop(0)))
        if par == 1:
            acc = jnp.where(even_rows, halves[0], halves[1])
            halves = []
            for pp in range(n_pairs):
                c0 = (kh * n_pairs + pp) * PAIR
                o_ref[:, c0:c0 + PAIR] = acc[:, pp * ATT_BLOCK:(pp + 1) * ATT_BLOCK].T.astype(bf16)


def _band_bias():
    kj = np.arange(BAND)[:, None]
    qi = np.arange(KV_GROUP // 2 * ATT_BLOCK)[None, :] % ATT_BLOCK
    tabs = [np.where(np.abs(qi + rel - kj) <= WINDOW, 0.0, NEG_INF) for rel in (0, ATT_BLOCK, 2 * ATT_BLOCK)]
    return jnp.asarray(np.stack(tabs), f32)


def _attention(sink, qt, k_ext, vt_ext, j, with_ctx_rows):
    ctx_blk = T_LAT // CTX_LEN
    ctx_sub = CTX_LEN // ATT_BLOCK
    nb = SEQ // ATT_BLOCK
    ctx_steps = ctx_sub if with_ctx_rows else 0
    tok_blk = lambda b, n: jnp.where(n < nb, b * nb + n, T_LAT // ATT_BLOCK + b * ctx_sub + n - nb)
    bias_row = lambda b, n: jnp.where(n == 0, 0, jnp.where(n == nb - 1, 2, 1))
    return pl.pallas_call(
        functools.partial(_attn_kernel, layer=j, ctx_steps=ctx_steps),
        grid=(BATCH, nb + ctx_steps),
        in_specs=[
            pl.BlockSpec(memory_space=pltpu.SMEM),
            pl.BlockSpec((D, ATT_BLOCK), lambda b, n: (0, tok_blk(b, n))),
            pl.BlockSpec((CTX_LEN, KV_EXT), lambda b, n: (ctx_blk + b, 0)),
            pl.BlockSpec((ctx_sub, KV_EXT, ATT_BLOCK), lambda b, n: (ctx_blk + b, 0, 0)),
            pl.BlockSpec((1, BAND, KV_GROUP // 2 * ATT_BLOCK), lambda b, n: (bias_row(b, n), 0, 0)),
            pl.BlockSpec((SEQ, KV_EXT), lambda b, n: (b, 0)),
            pl.BlockSpec((nb, KV_EXT, ATT_BLOCK), lambda b, n: (b, 0, 0)),
        ],
        out_specs=pl.BlockSpec((ATT_BLOCK, D), lambda b, n: (tok_blk(b, n), 0)),
        out_shape=jax.ShapeDtypeStruct((T_ALL if with_ctx_rows else T_LAT, D), bf16),
        compiler_params=_params(2),
        name="attention",
    )(sink, qt, k_ext, vt_ext, _band_bias(), k_ext, vt_ext)


def _proj_kernel(x_ref, o_ref_in, mod_ref, ng_ref, w_ref, out_ref):
    n_sub = x_ref.shape[0] // MIX_SUB
    ys = [_dot(o_ref_in[s * MIX_SUB:(s + 1) * MIX_SUB, :], w_ref[0]) for s in range(n_sub)]
    for s in range(n_sub):
        rows = slice(s * MIX_SUB, (s + 1) * MIX_SUB)
        out_ref[rows, :] = x_ref[rows, :] + mod_ref[0, 0][2:3] * _rms(ys[s], ng_ref[0][1:2])


def _proj_layer(x_all, o_all, mod, norm_g, w_o, i, j, n_tok):
    tm = TM_MIX
    row = _mod_row(tm)
    return pl.pallas_call(
        _proj_kernel,
        grid=(n_tok // tm,),
        in_specs=[
            pl.BlockSpec((tm, D), lambda t: (t, 0)),
            pl.BlockSpec((tm, D), lambda t: (t, 0)),
            pl.BlockSpec((1, 1, N_MOD, D), lambda t: (i, row(t), 0, 0)),
            pl.BlockSpec((1, 4, D), lambda t: (i, 0, 0)),
            _resident((1, D, D), lambda t: (j, 0, 0)),
        ],
        out_specs=pl.BlockSpec((tm, D), lambda t: (t, 0)),
        out_shape=jax.ShapeDtypeStruct((n_tok, D), f32),
        compiler_params=_params(1),
        name="attn_proj",
    )(x_all, o_all, mod, norm_g, w_o)


def _row_loop(n_rows, body):
    def step(r, carry):
        body(pl.ds(pl.multiple_of(r * ROW_CHUNK, ROW_CHUNK), ROW_CHUNK))
        return carry
    lax.fori_loop(0, n_rows // ROW_CHUNK, step, 0, unroll=ROW_UNROLL)


def _ffn_kernel(*refs, n_cast):
    x_ref, mod_ref, ng_ref, wg_ref, wu_ref, wd_ref = refs[:6]
    o_ref = refs[6 + n_cast]
    h_scr, acc_scr = refs[-2:]
    hstep = pl.program_id(1)
    tm = x_ref.shape[0]
    for src_ref, dst_ref in zip(refs[6:6 + n_cast], refs[7 + n_cast:7 + 2 * n_cast]):
        dst_ref[...] = src_ref[...].astype(bf16)

    @pl.when(hstep == 0)
    def _():
        mod = mod_ref[0, 0]
        gain = ng_ref[0][2:3] * (1.0 + mod[4:5])
        shift = mod[3:4]

        def body(rows):
            xr = x_ref[rows, :]
            inv = lax.rsqrt(jnp.mean(xr * xr, axis=-1, keepdims=True) + EPS)
            h_scr[rows, :] = (xr * inv * gain + shift).astype(bf16)
            acc_scr[rows, :] = jnp.zeros((ROW_CHUNK, D), f32)
        _row_loop(tm, body)

    h = h_scr[...]
    g = _dot(h, wg_ref[0].astype(bf16))
    u = _dot(h, wu_ref[0].astype(bf16))
    a = (g * jax.nn.sigmoid(g) * u).astype(bf16)
    wd = wd_ref[0].astype(bf16)
    for c in range(D // FFN_DOWN_COLS):
        cols = slice(c * FFN_DOWN_COLS, (c + 1) * FFN_DOWN_COLS)
        acc_scr[:, cols] += _dot(a, wd[:, cols])

    @pl.when(hstep == pl.num_programs(1) - 1)
    def _():
        gain = mod_ref[0, 0][5:6] * ng_ref[0][3:4]

        def body(rows):
            y = acc_scr[rows, :]
            inv = lax.rsqrt(jnp.mean(y * y, axis=-1, keepdims=True) + EPS)
            o_ref[rows, :] = x_ref[rows, :] + y * inv * gain
        _row_loop(tm, body)


def _ffn_layer(x_all, mod, norm_g, w_gu, w_down, i, n_tok, cast=()):
    tm, th = TM_FFN, TH_FFN
    row = _mod_row(tm)
    nh = FFN_HIDDEN // th
    n_tiles = n_tok // tm
    assert not cast or n_tiles * nh >= CAST_STEPS
    cast_blk = lambda t, h: jnp.minimum(t * nh + h, CAST_STEPS - 1)
    cast_in, cast_out, cast_shapes = [], [], []
    for w, layer in cast:
        _, rows, cols = w.shape
        blk = (1, rows // CAST_STEPS, cols)
        cast_in.append(pl.BlockSpec(blk, lambda t, h, layer=layer: (layer, cast_blk(t, h), 0)))
        cast_out.append(pl.BlockSpec(blk, lambda t, h: (0, cast_blk(t, h), 0)))
        cast_shapes.append(jax.ShapeDtypeStruct((1, rows, cols), bf16))
    outs = pl.pallas_call(
        functools.partial(_ffn_kernel, n_cast=len(cast)),
        grid=(n_tiles, nh),
        in_specs=[
            pl.BlockSpec((tm, D), lambda t, h: (t, 0)),
            pl.BlockSpec((1, 1, N_MOD, D), lambda t, h: (i, row(t), 0, 0)),
            pl.BlockSpec((1, 4, D), lambda t, h: (i, 0, 0)),
            pl.BlockSpec((1, D, th), lambda t, h: (i, 0, h)),
            pl.BlockSpec((1, D, th), lambda t, h: (i, 0, nh + h)),
            pl.BlockSpec((1, th, D), lambda t, h: (i, h, 0)),
        ] + cast_in,
        out_specs=[pl.BlockSpec((tm, D), lambda t, h: (t, 0))] + cast_out,
        out_shape=[jax.ShapeDtypeStruct((n_tok, D), f32)] + cast_shapes,
        scratch_shapes=[pltpu.VMEM((tm, D), bf16), pltpu.VMEM((tm, D), f32)],
        compiler_params=_params(2, BIG_VMEM_LIMIT),
        name="swiglu_ffn",
    )(x_all, mod, norm_g, w_gu, w_gu, w_down, *[w for w, _ in cast])
    return outs[0], tuple(outs[1:])


def _rope_tables():
    rows = SEQ // GRID_W
    pos = np.stack([np.repeat(np.arange(rows), GRID_W), np.tile(np.arange(GRID_W), rows)], -1)
    n_freq = HEAD_DIM // 4
    inv = jnp.asarray(ROPE_THETA, f32) ** (-jnp.arange(n_freq, dtype=f32) / n_freq)
    ang = jnp.asarray(pos, f32)[:, :, None] * inv
    cos, sin = jnp.cos(ang), jnp.sin(ang)
    zero = jnp.zeros_like(sin)
    cos_h = jnp.concatenate([cos[:, 0], cos[:, 0], cos[:, 1], cos[:, 1]], -1)
    sa_h = jnp.concatenate([zero[:, 0], sin[:, 0], zero[:, 1], sin[:, 1]], -1)
    sb_h = jnp.concatenate([-sin[:, 0], zero[:, 0], -sin[:, 1], zero[:, 1]], -1)
    ident = (jnp.ones((TM_MIX, HEAD_DIM), f32), jnp.zeros((TM_MIX, HEAD_DIM), f32),
             jnp.zeros((TM_MIX, HEAD_DIM), f32))
    return tuple(jnp.tile(jnp.concatenate([t, e], 0), (1, LANES // HEAD_DIM))
                 for t, e in zip((cos_h, sa_h, sb_h), ident))


def kernel(x, c, ctx, c_ctx, w_mod, b_mod, norm_g, a_w_in, a_v_norm_g, a_w_s, a_b_s, a_w_out,
           b_w_qkv, b_sink, b_w_o, f_w_gu, f_w_down):
    assert x.shape == (BATCH, SEQ, D) and ctx.shape == (BATCH, CTX_LEN, D)
    xs = (x.reshape(T_LAT, D), ctx.reshape(T_CTX, D))
    c_all = jnp.concatenate(
        [c, c_ctx[None], jnp.zeros((MOD_ROWS - BATCH - 1, D), f32)], axis=0)
    mod = _modulation(c_all, w_mod, b_mod).reshape(DEPTH, MOD_ROWS, N_MOD, D)

    w_s = a_w_s.astype(bf16)
    b_full = jnp.repeat(jnp.swapaxes(a_b_s, 1, 2), GROUP_DIM, axis=2)
    v_g = a_v_norm_g.reshape(-1, 1, D)
    rope = _rope_tables()

    mixer_w = (a_w_in[:1].astype(bf16), a_w_out[:1].astype(bf16))
    for i in range(DEPTH):
        last = i == DEPTH - 1
        j = i // 2
        n_tok = T_LAT if last else T_ALL
        if i % 2 == 0:
            x_all = _gmlp_layer(xs if i == 0 else (x_all,), mod, norm_g, mixer_w[0], v_g[j:j + 1], w_s[j:j + 1],
                                b_full[j:j + 1], mixer_w[1], i, 0, n_tok)
        else:
            qt, k_ext, vt_ext = _qkv_layer(x_all, mod, norm_g, mixer_w[0], rope, i, 0)
            o = _attention(b_sink, qt, k_ext, vt_ext, j, with_ctx_rows=not last)
            x_all = _proj_layer(x_all, o, mod, norm_g, mixer_w[1], i, 0, n_tok)
        if last:
            cast = ()
        elif i % 2 == 0:
            cast = ((b_w_qkv, j), (b_w_o, j))
        else:
            cast = ((a_w_in, j + 1), (a_w_out, j + 1))
        x_all, mixer_w = _ffn_layer(x_all, mod, norm_g, f_w_gu, f_w_down, i, n_tok, cast)
    return x_all.reshape(BATCH, SEQ, D)
```

```python
import functools

import jax
import jax.numpy as jnp
import numpy as np
from jax import lax
from jax.experimental import pallas as pl
from jax.experimental.pallas import tpu as pltpu

D = 2048
BATCH = 4
SEQ = 2048
DEPTH = 4
GRID_W = 64
CTX_LEN = 256
CHUNK = 128
GROUP_DIM = 128
GROUPS = D // GROUP_DIM
HEAD_DIM = 64
N_Q_HEADS = D // HEAD_DIM
KV_GROUP = 8
N_KV_HEADS = N_Q_HEADS // KV_GROUP
WINDOW = 128
ATT_BLOCK = 128
BAND = 3 * ATT_BLOCK
ROPE_THETA = 10000.0
FFN_HIDDEN = 5632
N_MOD = 6
EPS = 1e-6
NEG_INF = -1e30

T_LAT = BATCH * SEQ
T_CTX = BATCH * CTX_LEN
T_ALL = T_LAT + T_CTX
MOD_ROWS = 8
LANES = 128
LOG2E = float(np.log2(np.e))
PAIR = 2 * HEAD_DIM
KV_EXT = N_KV_HEADS * 2 * PAIR

TM_MIX = 512
MIX_SUB = 256
TM_GMLP = 256
GMLP_SUB = 256
TM_FFN = 1024
TH_FFN = 256
CAST_STEPS = 128
FFN_DOWN_COLS = 512
ROW_CHUNK = 16
ROW_UNROLL = 16
ATTN_LOOKAHEAD = 2
TN_MOD = 1024
VMEM_LIMIT = 56 * 1024 * 1024
BIG_VMEM_LIMIT = 60 * 1024 * 1024

bf16 = jnp.bfloat16
f32 = jnp.float32


def _rms(x, g):
    return x * lax.rsqrt(jnp.mean(x * x, axis=-1, keepdims=True) + EPS) * g


def _dot(a, b):
    return jnp.dot(a, b, preferred_element_type=f32)


def _dot_nt(a, b):
    return lax.dot_general(a, b, (((1,), (1,)), ((), ())), preferred_element_type=f32)


def _mod_row(tm):
    return lambda t: jnp.minimum((t * tm) // SEQ, BATCH)


def _params(n_axes, vmem=VMEM_LIMIT):
    return pltpu.CompilerParams(
        dimension_semantics=("arbitrary",) * n_axes, vmem_limit_bytes=vmem)


def _resident(shape, index_map):
    return pl.BlockSpec(shape, index_map, pipeline_mode=pl.Buffered(1))


def _mod_kernel(c_ref, w_ref, b_ref, o_ref):
    cc = c_ref[...]
    s = (cc * jax.nn.sigmoid(cc)).astype(bf16)
    o_ref[0] = _dot(s, w_ref[0].astype(bf16)) + b_ref[0]


def _modulation(c_all, w_mod, b_mod):
    n = N_MOD * D
    return pl.pallas_call(
        _mod_kernel,
        grid=(DEPTH, n // TN_MOD),
        in_specs=[
            pl.BlockSpec((MOD_ROWS, D), lambda i, j: (0, 0)),
            pl.BlockSpec((1, D, TN_MOD), lambda i, j: (i, 0, j)),
            pl.BlockSpec((1, 1, TN_MOD), lambda i, j: (i, 0, j)),
        ],
        out_specs=pl.BlockSpec((1, MOD_ROWS, TN_MOD), lambda i, j: (i, 0, j)),
        out_shape=jax.ShapeDtypeStruct((DEPTH, MOD_ROWS, n), f32),
        compiler_params=_params(2),
        name="modulation",
    )(c_all, w_mod, b_mod.reshape(DEPTH, 1, n))


def _gmlp_kernel(x_ref, *rest, split_input):
    if split_input:
        c_ref, *rest = rest
    mod_ref, ng_ref, win_ref, vg_ref, ws_ref, bs_ref, wout_ref, o_ref = rest
    mod = mod_ref[0, 0]
    ng = ng_ref[0]
    n_sub = TM_GMLP // GMLP_SUB
    n_chunks = GMLP_SUB // CHUNK
    xs, zs = [], []
    for r in range(n_sub):
        rows = slice(r * GMLP_SUB, (r + 1) * GMLP_SUB)
        if split_input:
            x = jnp.where(pl.program_id(0) < T_LAT // TM_GMLP, x_ref[rows, :], c_ref[rows, :])
        else:
            x = x_ref[rows, :]
        h = _rms(x, ng[0:1]) * (1.0 + mod[1:2]) + mod[0:1]
        xs.append(x)
        hb = h.astype(bf16)
        zs.append((_dot(hb, win_ref[0, :, D:]), _dot(hb, win_ref[0, :, :D])))
    gelu = lambda z: 0.5 * z * (1.0 + lax.erf(z * np.float32(np.sqrt(0.5))))
    for r in range(n_sub):
        v = gelu(zs[r][0])
        u = gelu(zs[r][1])
        mu = jnp.mean(v, axis=-1, keepdims=True)
        vc = v - mu
        var = jnp.mean(vc * vc, axis=-1, keepdims=True)
        vn = (vc * lax.rsqrt(var + EPS) * vg_ref[0]).astype(bf16)
        cols = []
        for g in range(GROUPS):
            c0 = g * GROUP_DIM
            vg = jnp.concatenate(
                [vn[ck * CHUNK:(ck + 1) * CHUNK, c0:c0 + GROUP_DIM] for ck in range(n_chunks)], axis=1)
            sg = _dot(ws_ref[0, g], vg)
            cols.append(jnp.concatenate(
                [sg[:, ck * GROUP_DIM:(ck + 1) * GROUP_DIM] for ck in range(n_chunks)], axis=0))
        s = jnp.concatenate(cols, axis=1) + jnp.concatenate([bs_ref[0]] * n_chunks, axis=0)
        y = _dot((u * s).astype(bf16), wout_ref[0])
        o_ref[r * GMLP_SUB:(r + 1) * GMLP_SUB, :] = xs[r] + mod[2:3] * _rms(y, ng[1:2])


def _gmlp_layer(xs, mod, norm_g, w_in, v_g, w_s, b_full, w_out, i, j, n_tok):
    tm = TM_GMLP
    row = _mod_row(tm)
    n_lat = T_LAT // tm
    if len(xs) == 2:
        x_specs = [pl.BlockSpec((tm, D), lambda t: (jnp.minimum(t, n_lat - 1), 0)),
                   pl.BlockSpec((tm, D), lambda t: (jnp.maximum(t - n_lat, 0), 0))]
    else:
        x_specs = [pl.BlockSpec((tm, D), lambda t: (t, 0))]
    return pl.pallas_call(
        functools.partial(_gmlp_kernel, split_input=len(xs) == 2),
        grid=(n_tok // tm,),
        in_specs=x_specs + [
            pl.BlockSpec((1, 1, N_MOD, D), lambda t: (i, row(t), 0, 0)),
            pl.BlockSpec((1, 4, D), lambda t: (i, 0, 0)),
            _resident((1, D, 2 * D), lambda t: (j, 0, 0)),
            pl.BlockSpec((1, 1, D), lambda t: (j, 0, 0)),
            _resident((1, GROUPS, CHUNK, CHUNK), lambda t: (j, 0, 0, 0)),
            _resident((1, CHUNK, D), lambda t: (j, 0, 0)),
            _resident((1, D, D), lambda t: (j, 0, 0)),
        ],
        out_specs=pl.BlockSpec((tm, D), lambda t: (t, 0)),
        out_shape=jax.ShapeDtypeStruct((n_tok, D), f32),
        compiler_params=_params(1, BIG_VMEM_LIMIT),
        name="gmlp_mixer",
    )(*xs, mod, norm_g, w_in, v_g, w_s, b_full, w_out)


def _rope(xt, cos, sa, sb):
    return xt * cos + pltpu.roll(xt, 16, 1) * sa + pltpu.roll(xt, LANES - 16, 1) * sb


def _qkv_kernel(x_ref, mod_ref, ng_ref, w_ref, cos_ref, sa_ref, sb_ref, qt_ref, k_ref, vt_ref):
    mod = mod_ref[0, 0]
    n_sub = x_ref.shape[0] // MIX_SUB
    qkvs = []
    for s in range(n_sub):
        x = x_ref[s * MIX_SUB:(s + 1) * MIX_SUB, :]
        h = _rms(x, ng_ref[0][0:1]) * (1.0 + mod[1:2]) + mod[0:1]
        qkvs.append(_dot(h.astype(bf16), w_ref[0]))
    scale = np.float32(HEAD_DIM ** -0.5 * LOG2E)
    n_blk = MIX_SUB // ATT_BLOCK
    lane = lax.broadcasted_iota(jnp.int32, (MIX_SUB, LANES), 1)
    lo = lane < HEAD_DIM
    for s in range(n_sub):
        qkv = qkvs[s]
        rows = slice(s * MIX_SUB, (s + 1) * MIX_SUB)
        cos, sa, sb = cos_ref[rows, :], sa_ref[rows, :], sb_ref[rows, :]
        for p in range(D // LANES):
            c0 = p * LANES
            qt = _rope(qkv[:, c0:c0 + LANES], cos, sa, sb) * scale
            for r in range(n_blk):
                r0 = r * ATT_BLOCK
                qt_ref[s * n_blk + r, c0:c0 + LANES, :] = qt[r0:r0 + ATT_BLOCK].T.astype(bf16)
        for m in range(N_KV_HEADS // 2):
            kt = _rope(qkv[:, D + m * LANES:D + (m + 1) * LANES], cos, sa, sb)
            vt = qkv[:, D + N_KV_HEADS * HEAD_DIM + m * LANES:D + N_KV_HEADS * HEAD_DIM + (m + 1) * LANES]
            base = 2 * m * 2 * PAIR
            sw = pltpu.roll(kt, HEAD_DIM, 1)
            k_ref[rows, base:base + PAIR] = jnp.where(lo, kt, 0.0).astype(bf16)
            k_ref[rows, base + PAIR:base + 2 * PAIR] = jnp.where(lo, 0.0, sw).astype(bf16)
            k_ref[rows, base + 2 * PAIR:base + 3 * PAIR] = jnp.where(lo, sw, 0.0).astype(bf16)
            k_ref[rows, base + 3 * PAIR:base + 4 * PAIR] = jnp.where(lo, 0.0, kt).astype(bf16)
            sw = pltpu.roll(vt, HEAD_DIM, 1)
            one_hi = (lane == HEAD_DIM).astype(f32)
            one_lo = (lane == 0).astype(f32)
            parts = (jnp.where(lo, vt, one_hi), jnp.where(lo, one_lo, sw),
                     jnp.where(lo, sw, one_hi), jnp.where(lo, one_lo, vt))
            for e, part in enumerate(parts):
                for r in range(n_blk):
                    r0 = r * ATT_BLOCK
                    vt_ref[s * n_blk + r, base + e * PAIR:base + (e + 1) * PAIR, :] = (
                        part[r0:r0 + ATT_BLOCK].T.astype(bf16))


def _qkv_layer(x_all, mod, norm_g, w_qkv, rope, i, j):
    tm = TM_MIX
    row = _mod_row(tm)
    n_lat_tiles = SEQ // tm
    n_blk = tm // ATT_BLOCK
    rope_row = lambda t: (jnp.where(t < T_LAT // tm, t % n_lat_tiles, n_lat_tiles), 0)
    n_qkv = w_qkv.shape[-1]
    return pl.pallas_call(
        _qkv_kernel,
        grid=(T_ALL // tm,),
        in_specs=[
            pl.BlockSpec((tm, D), lambda t: (t, 0)),
            pl.BlockSpec((1, 1, N_MOD, D), lambda t: (i, row(t), 0, 0)),
            pl.BlockSpec((1, 4, D), lambda t: (i, 0, 0)),
            _resident((1, D, n_qkv), lambda t: (j, 0, 0)),
            pl.BlockSpec((tm, LANES), rope_row),
            pl.BlockSpec((tm, LANES), rope_row),
            pl.BlockSpec((tm, LANES), rope_row),
        ],
        out_specs=[
            pl.BlockSpec((n_blk, D, ATT_BLOCK), lambda t: (t, 0, 0)),
            pl.BlockSpec((tm, KV_EXT), lambda t: (t, 0)),
            pl.BlockSpec((n_blk, KV_EXT, ATT_BLOCK), lambda t: (t, 0, 0)),
        ],
        out_shape=[
            jax.ShapeDtypeStruct((T_ALL // ATT_BLOCK, D, ATT_BLOCK), bf16),
            jax.ShapeDtypeStruct((T_ALL, KV_EXT), bf16),
            jax.ShapeDtypeStruct((T_ALL // ATT_BLOCK, KV_EXT, ATT_BLOCK), bf16),
        ],
        compiler_params=_params(1),
        name="qkv_rope",
    )(x_all, mod, norm_g, w_qkv, *rope)


def _attn_kernel(sink_ref, qt_ref, kc_ref, vct_ref, bias_ref, kl_ref, vlt_ref, o_ref, *, layer, ctx_steps):
    n = pl.program_id(1)

    @pl.when(n < SEQ // ATT_BLOCK)
    def _():
        _attn_block(sink_ref, qt_ref, kc_ref, vct_ref, bias_ref, kl_ref, vlt_ref, o_ref, layer, True)

    if ctx_steps:
        @pl.when(n >= SEQ // ATT_BLOCK)
        def _():
            _attn_block(sink_ref, qt_ref, kc_ref, vct_ref, None, None, None, o_ref, layer, False)


def _attn_block(sink_ref, qt_ref, kc_ref, vct_ref, bias_ref, kl_ref, vlt_ref, o_ref, layer, with_band):
    n_pairs = KV_GROUP // 2
    n_cols = n_pairs * ATT_BLOCK
    if with_band:
        blk0 = jnp.clip(pl.program_id(1) - 1, 0, (SEQ - BAND) // ATT_BLOCK)
        s0 = pl.multiple_of(blk0 * ATT_BLOCK, ATT_BLOCK)
    even_rows = lax.broadcasted_iota(jnp.int32, (PAIR, n_cols), 0) < HEAD_DIM

    def scores(kh, par):
        qt = jnp.concatenate(
            [qt_ref[0, (kh * n_pairs + pp) * PAIR:(kh * n_pairs + pp + 1) * PAIR, :] for pp in range(n_pairs)],
            axis=1)
        c0 = (2 * kh + par) * PAIR
        s_ctx = _dot(kc_ref[:, c0:c0 + PAIR], qt)
        s_band = _dot(kl_ref[pl.ds(s0, BAND), c0:c0 + PAIR], qt) + bias_ref[0] if with_band else None
        return s_ctx, s_band

    def weighted_values(kh, par, s_ctx, s_band):
        c0 = (2 * kh + par) * PAIR
        sink = jnp.concatenate(
            [jnp.full((1, ATT_BLOCK), sink_ref[layer, kh * KV_GROUP + 2 * pp + par] * LOG2E, f32)
             for pp in range(n_pairs)], axis=1)
        m = jnp.maximum(jnp.max(s_ctx, axis=0, keepdims=True), sink)
        if with_band:
            m = jnp.maximum(m, jnp.max(s_band, axis=0, keepdims=True))
        vct = jnp.concatenate(
            [vct_ref[r, c0:c0 + PAIR, :] for r in range(CTX_LEN // ATT_BLOCK)], axis=1)
        o = _dot(vct, jnp.exp2(s_ctx - m).astype(bf16))
        if with_band:
            vlt = jnp.concatenate(
                [vlt_ref[blk0 + r, c0:c0 + PAIR, :] for r in range(BAND // ATT_BLOCK)], axis=1)
            o = o + _dot(vlt, jnp.exp2(s_band - m).astype(bf16))
        den_row = HEAD_DIM if par == 0 else 0
        den = o[den_row:den_row + 1, :] + jnp.exp2(sink - m)
        return o * (1.0 / den)

    order = [(kh, par) for kh in range(N_KV_HEADS) for par in range(2)]
    pending = [scores(*order[it]) for it in range(ATTN_LOOKAHEAD)]
    halves = []
    for it, (kh, par) in enumerate(order):
        if it + ATTN_LOOKAHEAD < len(order):
            pending.append(scores(*order[it + ATTN_LOOKAHEAD]))
        halves.append(weighted_values(kh, par, *pending.pop(0)))
        if par == 1:
            acc = jnp.where(even_rows, halves[0], halves[1])
            halves = []
            for pp in range(n_pairs):
                c0 = (kh * n_pairs + pp) * PAIR
                o_ref[:, c0:c0 + PAIR] = acc[:, pp * ATT_BLOCK:(pp + 1) * ATT_BLOCK].T.astype(bf16)


def _band_bias():
    kj = np.arange(BAND)[:, None]
    qi = np.arange(KV_GROUP // 2 * ATT_BLOCK)[None, :] % ATT_BLOCK
    tabs = [np.where(np.abs(qi + rel - kj) <= WINDOW, 0.0, NEG_INF) for rel in (0, ATT_BLOCK, 2 * ATT_BLOCK)]
    return jnp.asarray(np.stack(tabs), f32)


def _attention(sink, qt, k_ext, vt_ext, j, with_ctx_rows):
    ctx_blk = T_LAT // CTX_LEN
    ctx_sub = CTX_LEN // ATT_BLOCK
    nb = SEQ // ATT_BLOCK
    ctx_steps = ctx_sub if with_ctx_rows else 0
    tok_blk = lambda b, n: jnp.where(n < nb, b * nb + n, T_LAT // ATT_BLOCK + b * ctx_sub + n - nb)
    bias_row = lambda b, n: jnp.where(n == 0, 0, jnp.where(n == nb - 1, 2, 1))
    return pl.pallas_call(
        functools.partial(_attn_kernel, layer=j, ctx_steps=ctx_steps),
        grid=(BATCH, nb + ctx_steps),
        in_specs=[
            pl.BlockSpec(memory_space=pltpu.SMEM),
            pl.BlockSpec((1, D, ATT_BLOCK), lambda b, n: (tok_blk(b, n), 0, 0)),
            pl.BlockSpec((CTX_LEN, KV_EXT), lambda b, n: (ctx_blk + b, 0)),
            pl.BlockSpec((ctx_sub, KV_EXT, ATT_BLOCK), lambda b, n: (ctx_blk + b, 0, 0)),
            pl.BlockSpec((1, BAND, KV_GROUP // 2 * ATT_BLOCK), lambda b, n: (bias_row(b, n), 0, 0)),
            pl.BlockSpec((SEQ, KV_EXT), lambda b, n: (b, 0)),
            pl.BlockSpec((nb, KV_EXT, ATT_BLOCK), lambda b, n: (b, 0, 0)),
        ],
        out_specs=pl.BlockSpec((ATT_BLOCK, D), lambda b, n: (tok_blk(b, n), 0)),
        out_shape=jax.ShapeDtypeStruct((T_ALL if with_ctx_rows else T_LAT, D), bf16),
        compiler_params=_params(2),
        name="attention",
    )(sink, qt, k_ext, vt_ext, _band_bias(), k_ext, vt_ext)


def _proj_kernel(x_ref, o_ref_in, mod_ref, ng_ref, w_ref, out_ref):
    n_sub = x_ref.shape[0] // MIX_SUB
    ys = [_dot(o_ref_in[s * MIX_SUB:(s + 1) * MIX_SUB, :], w_ref[0]) for s in range(n_sub)]
    for s in range(n_sub):
        rows = slice(s * MIX_SUB, (s + 1) * MIX_SUB)
        out_ref[rows, :] = x_ref[rows, :] + mod_ref[0, 0][2:3] * _rms(ys[s], ng_ref[0][1:2])


def _proj_layer(x_all, o_all, mod, norm_g, w_o, i, j, n_tok):
    tm = TM_MIX
    row = _mod_row(tm)
    return pl.pallas_call(
        _proj_kernel,
        grid=(n_tok // tm,),
        in_specs=[
            pl.BlockSpec((tm, D), lambda t: (t, 0)),
            pl.BlockSpec((tm, D), lambda t: (t, 0)),
            pl.BlockSpec((1, 1, N_MOD, D), lambda t: (i, row(t), 0, 0)),
            pl.BlockSpec((1, 4, D), lambda t: (i, 0, 0)),
            _resident((1, D, D), lambda t: (j, 0, 0)),
        ],
        out_specs=pl.BlockSpec((tm, D), lambda t: (t, 0)),
        out_shape=jax.ShapeDtypeStruct((n_tok, D), f32),
        compiler_params=_params(1),
        name="attn_proj",
    )(x_all, o_all, mod, norm_g, w_o)


def _row_loop(n_rows, body):
    def step(r, carry):
        body(pl.ds(pl.multiple_of(r * ROW_CHUNK, ROW_CHUNK), ROW_CHUNK))
        return carry
    lax.fori_loop(0, n_rows // ROW_CHUNK, step, 0, unroll=ROW_UNROLL)


def _ffn_kernel(*refs, n_cast):
    x_ref, mod_ref, ng_ref, wg_ref, wu_ref, wd_ref = refs[:6]
    o_ref = refs[6 + n_cast]
    h_scr, acc_scr = refs[-2:]
    hstep = pl.program_id(1)
    tm = x_ref.shape[0]
    for src_ref, dst_ref in zip(refs[6:6 + n_cast], refs[7 + n_cast:7 + 2 * n_cast]):
        dst_ref[...] = src_ref[...].astype(bf16)

    @pl.when(hstep == 0)
    def _():
        mod = mod_ref[0, 0]
        gain = ng_ref[0][2:3] * (1.0 + mod[4:5])
        shift = mod[3:4]

        def body(rows):
            xr = x_ref[rows, :]
            inv = lax.rsqrt(jnp.mean(xr * xr, axis=-1, keepdims=True) + EPS)
            h_scr[rows, :] = (xr * inv * gain + shift).astype(bf16)
            acc_scr[rows, :] = jnp.zeros((ROW_CHUNK, D), f32)
        _row_loop(tm, body)

    h = h_scr[...]
    g = _dot(h, wg_ref[0].astype(bf16))
    u = _dot(h, wu_ref[0].astype(bf16))
    a = (g * jax.nn.sigmoid(g) * u).astype(bf16)
    wd = wd_ref[0].astype(bf16)
    for c in range(D // FFN_DOWN_COLS):
        cols = slice(c * FFN_DOWN_COLS, (c + 1) * FFN_DOWN_COLS)
        acc_scr[:, cols] += _dot(a, wd[:, cols])

    @pl.when(hstep == pl.num_programs(1) - 1)
    def _():
        gain = mod_ref[0, 0][5:6] * ng_ref[0][3:4]

        def body(rows):
            y = acc_scr[rows, :]
            inv = lax.rsqrt(jnp.mean(y * y, axis=-1, keepdims=True) + EPS)
            o_ref[rows, :] = x_ref[rows, :] + y * inv * gain
        _row_loop(tm, body)


def _ffn_layer(x_all, mod, norm_g, w_gu, w_down, i, n_tok, cast=()):
    tm, th = TM_FFN, TH_FFN
    row = _mod_row(tm)
    nh = FFN_HIDDEN // th
    n_tiles = n_tok // tm
    assert not cast or n_tiles * nh >= CAST_STEPS
    cast_blk = lambda t, h: jnp.minimum(t * nh + h, CAST_STEPS - 1)
    cast_in, cast_out, cast_shapes = [], [], []
    for w, layer in cast:
        _, rows, cols = w.shape
        blk = (1, rows // CAST_STEPS, cols)
        cast_in.append(pl.BlockSpec(blk, lambda t, h, layer=layer: (layer, cast_blk(t, h), 0)))
        cast_out.append(pl.BlockSpec(blk, lambda t, h: (0, cast_blk(t, h), 0)))
        cast_shapes.append(jax.ShapeDtypeStruct((1, rows, cols), bf16))
    outs = pl.pallas_call(
        functools.partial(_ffn_kernel, n_cast=len(cast)),
        grid=(n_tiles, nh),
        in_specs=[
            pl.BlockSpec((tm, D), lambda t, h: (t, 0)),
            pl.BlockSpec((1, 1, N_MOD, D), lambda t, h: (i, row(t), 0, 0)),
            pl.BlockSpec((1, 4, D), lambda t, h: (i, 0, 0)),
            pl.BlockSpec((1, D, th), lambda t, h: (i, 0, h)),
            pl.BlockSpec((1, D, th), lambda t, h: (i, 0, nh + h)),
            pl.BlockSpec((1, th, D), lambda t, h: (i, h, 0)),
        ] + cast_in,
        out_specs=[pl.BlockSpec((tm, D), lambda t, h: (t, 0))] + cast_out,
        out_shape=[jax.ShapeDtypeStruct((n_tok, D), f32)] + cast_shapes,
        scratch_shapes=[pltpu.VMEM((tm, D), bf16), pltpu.VMEM((tm, D), f32)],
        compiler_params=_params(2, BIG_VMEM_LIMIT),
        name="swiglu_ffn",
    )(x_all, mod, norm_g, w_gu, w_gu, w_down, *[w for w, _ in cast])
    return outs[0], tuple(outs[1:])


def _rope_tables():
    rows = SEQ // GRID_W
    pos = np.stack([np.repeat(np.arange(rows), GRID_W), np.tile(np.arange(GRID_W), rows)], -1)
    n_freq = HEAD_DIM // 4
    inv = jnp.asarray(ROPE_THETA, f32) ** (-jnp.arange(n_freq, dtype=f32) / n_freq)
    ang = jnp.asarray(pos, f32)[:, :, None] * inv
    cos, sin = jnp.cos(ang), jnp.sin(ang)
    zero = jnp.zeros_like(sin)
    cos_h = jnp.concatenate([cos[:, 0], cos[:, 0], cos[:, 1], cos[:, 1]], -1)
    sa_h = jnp.concatenate([zero[:, 0], sin[:, 0], zero[:, 1], sin[:, 1]], -1)
    sb_h = jnp.concatenate([-sin[:, 0], zero[:, 0], -sin[:, 1], zero[:, 1]], -1)
    ident = (jnp.ones((TM_MIX, HEAD_DIM), f32), jnp.zeros((TM_MIX, HEAD_DIM), f32),
             jnp.zeros((TM_MIX, HEAD_DIM), f32))
    return tuple(jnp.tile(jnp.concatenate([t, e], 0), (1, LANES // HEAD_DIM))
                 for t, e in zip((cos_h, sa_h, sb_h), ident))


def kernel(x, c, ctx, c_ctx, w_mod, b_mod, norm_g, a_w_in, a_v_norm_g, a_w_s, a_b_s, a_w_out,
           b_w_qkv, b_sink, b_w_o, f_w_gu, f_w_down):
    assert x.shape == (BATCH, SEQ, D) and ctx.shape == (BATCH, CTX_LEN, D)
    xs = (x.reshape(T_LAT, D), ctx.reshape(T_CTX, D))
    c_all = jnp.concatenate(
        [c, c_ctx[None], jnp.zeros((MOD_ROWS - BATCH - 1, D), f32)], axis=0)
    mod = _modulation(c_all, w_mod, b_mod).reshape(DEPTH, MOD_ROWS, N_MOD, D)

    w_s = a_w_s.astype(bf16)
    b_full = jnp.repeat(jnp.swapaxes(a_b_s, 1, 2), GROUP_DIM, axis=2)
    v_g = a_v_norm_g.reshape(-1, 1, D)
    rope = _rope_tables()

    mixer_w = (a_w_in[:1].astype(bf16), a_w_out[:1].astype(bf16))
    for i in range(DEPTH):
        last = i == DEPTH - 1
        j = i // 2
        n_tok = T_LAT if last else T_ALL
        if i % 2 == 0:
            x_all = _gmlp_layer(xs if i == 0 else (x_all,), mod, norm_g, mixer_w[0], v_g[j:j + 1], w_s[j:j + 1],
                                b_full[j:j + 1], mixer_w[1], i, 0, n_tok)
        else:
            qt, k_ext, vt_ext = _qkv_layer(x_all, mod, norm_g, mixer_w[0], rope, i, 0)
            o = _attention(b_sink, qt, k_ext, vt_ext, j, with_ctx_rows=not last)
            x_all = _proj_layer(x_all, o, mod, norm_g, mixer_w[1], i, 0, n_tok)
        if last:
            cast = ()
        elif i % 2 == 0:
            cast = ((b_w_qkv, j), (b_w_o, j))
        else:
            cast = ((a_w_in, j + 1), (a_w_out, j + 1))
        x_all, mixer_w = _ffn_layer(x_all, mod, norm_g, f_w_gu, f_w_down, i, n_tok, cast)
    return x_all.reshape(BATCH, SEQ, D)
```

```python
import functools

import jax
import jax.numpy as jnp
import numpy as np
from jax import lax
from jax.experimental import pallas as pl
from jax.experimental.pallas import tpu as pltpu

D = 2048
BATCH = 4
SEQ = 2048
DEPTH = 4
GRID_W = 64
CTX_LEN = 256
CHUNK = 128
GROUP_DIM = 128
GROUPS = D // GROUP_DIM
HEAD_DIM = 64
N_Q_HEADS = D // HEAD_DIM
KV_GROUP = 8
N_KV_HEADS = N_Q_HEADS // KV_GROUP
WINDOW = 128
ATT_BLOCK = 128
BAND = 3 * ATT_BLOCK
ROPE_THETA = 10000.0
FFN_HIDDEN = 5632
N_MOD = 6
EPS = 1e-6
NEG_INF = -1e30

T_LAT = BATCH * SEQ
T_CTX = BATCH * CTX_LEN
T_ALL = T_LAT + T_CTX
MOD_ROWS = 8
LANES = 128
LOG2E = float(np.log2(np.e))
PAIR = 2 * HEAD_DIM
KV_EXT = N_KV_HEADS * 2 * PAIR

TM_MIX = 512
MIX_SUB = 256
TM_GMLP = 256
GMLP_SUB = 256
TM_FFN = 1024
TH_FFN = 256
CAST_STEPS = 128
FFN_DOWN_COLS = 512
ROW_CHUNK = 16
ROW_UNROLL = 16
ATTN_LOOKAHEAD = 2
TN_MOD = 1024
VMEM_LIMIT = 56 * 1024 * 1024
BIG_VMEM_LIMIT = 60 * 1024 * 1024

bf16 = jnp.bfloat16
f32 = jnp.float32


def _rms(x, g):
    return x * lax.rsqrt(jnp.mean(x * x, axis=-1, keepdims=True) + EPS) * g


def _dot(a, b):
    return jnp.dot(a, b, preferred_element_type=f32)


def _mod_row(tm):
    return lambda t: jnp.minimum((t * tm) // SEQ, BATCH)


def _params(n_axes, vmem=VMEM_LIMIT):
    return pltpu.CompilerParams(
        dimension_semantics=("arbitrary",) * n_axes, vmem_limit_bytes=vmem)


def _resident(shape, index_map):
    return pl.BlockSpec(shape, index_map, pipeline_mode=pl.Buffered(1))


def _mod_kernel(c_ref, w_ref, b_ref, o_ref):
    cc = c_ref[...]
    s = (cc * jax.nn.sigmoid(cc)).astype(bf16)
    o_ref[0] = _dot(s, w_ref[0].astype(bf16)) + b_ref[0]


def _modulation(c_all, w_mod, b_mod):
    n = N_MOD * D
    return pl.pallas_call(
        _mod_kernel,
        grid=(DEPTH, n // TN_MOD),
        in_specs=[
            pl.BlockSpec((MOD_ROWS, D), lambda i, j: (0, 0)),
            pl.BlockSpec((1, D, TN_MOD), lambda i, j: (i, 0, j)),
            pl.BlockSpec((1, 1, TN_MOD), lambda i, j: (i, 0, j)),
        ],
        out_specs=pl.BlockSpec((1, MOD_ROWS, TN_MOD), lambda i, j: (i, 0, j)),
        out_shape=jax.ShapeDtypeStruct((DEPTH, MOD_ROWS, n), f32),
        compiler_params=_params(2),
        name="modulation",
    )(c_all, w_mod, b_mod.reshape(DEPTH, 1, n))


def _gmlp_kernel(x_ref, *rest, split_input):
    if split_input:
        c_ref, *rest = rest
    mod_ref, ng_ref, win_ref, vg_ref, ws_ref, bs_ref, wout_ref, o_ref = rest
    mod = mod_ref[0, 0]
    ng = ng_ref[0]
    n_sub = TM_GMLP // GMLP_SUB
    n_chunks = GMLP_SUB // CHUNK
    xs, zs = [], []
    for r in range(n_sub):
        rows = slice(r * GMLP_SUB, (r + 1) * GMLP_SUB)
        if split_input:
            x = jnp.where(pl.program_id(0) < T_LAT // TM_GMLP, x_ref[rows, :], c_ref[rows, :])
        else:
            x = x_ref[rows, :]
        h = _rms(x, ng[0:1]) * (1.0 + mod[1:2]) + mod[0:1]
        xs.append(x)
        hb = h.astype(bf16)
        zs.append((_dot(hb, win_ref[0, :, D:]), _dot(hb, win_ref[0, :, :D])))
    gelu = lambda z: 0.5 * z * (1.0 + lax.erf(z * np.float32(np.sqrt(0.5))))
    for r in range(n_sub):
        v = gelu(zs[r][0])
        u = gelu(zs[r][1])
        mu = jnp.mean(v, axis=-1, keepdims=True)
        vc = v - mu
        var = jnp.mean(vc * vc, axis=-1, keepdims=True)
        vn = (vc * lax.rsqrt(var + EPS) * vg_ref[0]).astype(bf16)
        cols = []
        for g in range(GROUPS):
            c0 = g * GROUP_DIM
            vg = jnp.concatenate(
                [vn[ck * CHUNK:(ck + 1) * CHUNK, c0:c0 + GROUP_DIM] for ck in range(n_chunks)], axis=1)
            sg = _dot(ws_ref[0, g], vg)
            cols.append(jnp.concatenate(
                [sg[:, ck * GROUP_DIM:(ck + 1) * GROUP_DIM] for ck in range(n_chunks)], axis=0))
        s = jnp.concatenate(cols, axis=1) + jnp.concatenate([bs_ref[0]] * n_chunks, axis=0)
        y = _dot((u * s).astype(bf16), wout_ref[0])
        o_ref[r * GMLP_SUB:(r + 1) * GMLP_SUB, :] = xs[r] + mod[2:3] * _rms(y, ng[1:2])


def _gmlp_layer(xs, mod, norm_g, w_in, v_g, w_s, b_full, w_out, i, j, n_tok):
    tm = TM_GMLP
    row = _mod_row(tm)
    n_lat = T_LAT // tm
    if len(xs) == 2:
        x_specs = [pl.BlockSpec((tm, D), lambda t: (jnp.minimum(t, n_lat - 1), 0)),
                   pl.BlockSpec((tm, D), lambda t: (jnp.maximum(t - n_lat, 0), 0))]
    else:
        x_specs = [pl.BlockSpec((tm, D), lambda t: (t, 0))]
    return pl.pallas_call(
        functools.partial(_gmlp_kernel, split_input=len(xs) == 2),
        grid=(n_tok // tm,),
        in_specs=x_specs + [
            pl.BlockSpec((1, 1, N_MOD, D), lambda t: (i, row(t), 0, 0)),
            pl.BlockSpec((1, 4, D), lambda t: (i, 0, 0)),
            _resident((1, D, 2 * D), lambda t: (j, 0, 0)),
            pl.BlockSpec((1, 1, D), lambda t: (j, 0, 0)),
            _resident((1, GROUPS, CHUNK, CHUNK), lambda t: (j, 0, 0, 0)),
            _resident((1, CHUNK, D), lambda t: (j, 0, 0)),
            _resident((1, D, D), lambda t: (j, 0, 0)),
        ],
        out_specs=pl.BlockSpec((tm, D), lambda t: (t, 0)),
        out_shape=jax.ShapeDtypeStruct((n_tok, D), f32),
        compiler_params=_params(1, BIG_VMEM_LIMIT),
        name="gmlp_mixer",
    )(*xs, mod, norm_g, w_in, v_g, w_s, b_full, w_out)


def _rope(xt, cos, sa, sb):
    return xt * cos + pltpu.roll(xt, 16, 1) * sa + pltpu.roll(xt, LANES - 16, 1) * sb


def _qkv_kernel(x_ref, mod_ref, ng_ref, w_ref, cos_ref, sa_ref, sb_ref, qt_ref, k_ref, vt_ref):
    mod = mod_ref[0, 0]
    n_sub = x_ref.shape[0] // MIX_SUB
    qkvs = []
    for s in range(n_sub):
        x = x_ref[s * MIX_SUB:(s + 1) * MIX_SUB, :]
        h = _rms(x, ng_ref[0][0:1]) * (1.0 + mod[1:2]) + mod[0:1]
        qkvs.append(_dot(h.astype(bf16), w_ref[0]))
    scale = np.float32(HEAD_DIM ** -0.5 * LOG2E)
    n_blk = MIX_SUB // ATT_BLOCK
    lane = lax.broadcasted_iota(jnp.int32, (MIX_SUB, LANES), 1)
    lo = lane < HEAD_DIM
    for s in range(n_sub):
        qkv = qkvs[s]
        rows = slice(s * MIX_SUB, (s + 1) * MIX_SUB)
        cos, sa, sb = cos_ref[rows, :], sa_ref[rows, :], sb_ref[rows, :]
        for p in range(D // LANES):
            c0 = p * LANES
            qt = _rope(qkv[:, c0:c0 + LANES], cos, sa, sb) * scale
            for r in range(n_blk):
                r0 = r * ATT_BLOCK
                qt_ref[s * n_blk + r, c0:c0 + LANES, :] = qt[r0:r0 + ATT_BLOCK].T.astype(bf16)
        for m in range(N_KV_HEADS // 2):
            kt = _rope(qkv[:, D + m * LANES:D + (m + 1) * LANES], cos, sa, sb)
            vt = qkv[:, D + N_KV_HEADS * HEAD_DIM + m * LANES:D + N_KV_HEADS * HEAD_DIM + (m + 1) * LANES]
            base = 2 * m * 2 * PAIR
            sw = pltpu.roll(kt, HEAD_DIM, 1)
            k_ref[rows, base:base + PAIR] = jnp.where(lo, kt, 0.0).astype(bf16)
            k_ref[rows, base + PAIR:base + 2 * PAIR] = jnp.where(lo, 0.0, sw).astype(bf16)
            k_ref[rows, base + 2 * PAIR:base + 3 * PAIR] = jnp.where(lo, sw, 0.0).astype(bf16)
            k_ref[rows, base + 3 * PAIR:base + 4 * PAIR] = jnp.where(lo, 0.0, kt).astype(bf16)
            sw = pltpu.roll(vt, HEAD_DIM, 1)
            one_hi = (lane == HEAD_DIM).astype(f32)
            one_lo = (lane == 0).astype(f32)
            parts = (jnp.where(lo, vt, one_hi), jnp.where(lo, one_lo, sw),
                     jnp.where(lo, sw, one_hi), jnp.where(lo, one_lo, vt))
            for e, part in enumerate(parts):
                for r in range(n_blk):
                    r0 = r * ATT_BLOCK
                    vt_ref[s * n_blk + r, base + e * PAIR:base + (e + 1) * PAIR, :] = (
                        part[r0:r0 + ATT_BLOCK].T.astype(bf16))


def _qkv_layer(x_all, mod, norm_g, w_qkv, rope, i, j):
    tm = TM_MIX
    row = _mod_row(tm)
    n_lat_tiles = SEQ // tm
    n_blk = tm // ATT_BLOCK
    rope_row = lambda t: (jnp.where(t < T_LAT // tm, t % n_lat_tiles, n_lat_tiles), 0)
    n_qkv = w_qkv.shape[-1]
    return pl.pallas_call(
        _qkv_kernel,
        grid=(T_ALL // tm,),
        in_specs=[
            pl.BlockSpec((tm, D), lambda t: (t, 0)),
            pl.BlockSpec((1, 1, N_MOD, D), lambda t: (i, row(t), 0, 0)),
            pl.BlockSpec((1, 4, D), lambda t: (i, 0, 0)),
            _resident((1, D, n_qkv), lambda t: (j, 0, 0)),
            pl.BlockSpec((tm, LANES), rope_row),
            pl.BlockSpec((tm, LANES), rope_row),
            pl.BlockSpec((tm, LANES), rope_row),
        ],
        out_specs=[
            pl.BlockSpec((n_blk, D, ATT_BLOCK), lambda t: (t, 0, 0)),
            pl.BlockSpec((tm, KV_EXT), lambda t: (t, 0)),
            pl.BlockSpec((n_blk, KV_EXT, ATT_BLOCK), lambda t: (t, 0, 0)),
        ],
        out_shape=[
            jax.ShapeDtypeStruct((T_ALL // ATT_BLOCK, D, ATT_BLOCK), bf16),
            jax.ShapeDtypeStruct((T_ALL, KV_EXT), bf16),
            jax.ShapeDtypeStruct((T_ALL // ATT_BLOCK, KV_EXT, ATT_BLOCK), bf16),
        ],
        compiler_params=_params(1),
        name="qkv_rope",
    )(x_all, mod, norm_g, w_qkv, *rope)


def _attn_kernel(sink_ref, qt_ref, kc_ref, vct_ref, bias_ref, kl_ref, vlt_ref, o_ref, *, layer, ctx_steps):
    n = pl.program_id(1)

    @pl.when(n < SEQ // ATT_BLOCK)
    def _():
        _attn_block(sink_ref, qt_ref, kc_ref, vct_ref, bias_ref, kl_ref, vlt_ref, o_ref, layer, True)

    if ctx_steps:
        @pl.when(n >= SEQ // ATT_BLOCK)
        def _():
            _attn_block(sink_ref, qt_ref, kc_ref, vct_ref, None, None, None, o_ref, layer, False)


def _attn_block(sink_ref, qt_ref, kc_ref, vct_ref, bias_ref, kl_ref, vlt_ref, o_ref, layer, with_band):
    n_pairs = KV_GROUP // 2
    n_cols = n_pairs * ATT_BLOCK
    if with_band:
        blk0 = jnp.clip(pl.program_id(1) - 1, 0, (SEQ - BAND) // ATT_BLOCK)
        s0 = pl.multiple_of(blk0 * ATT_BLOCK, ATT_BLOCK)
    even_rows = lax.broadcasted_iota(jnp.int32, (PAIR, n_cols), 0) < HEAD_DIM

    def scores(kh, par):
        qt = jnp.concatenate(
            [qt_ref[0, (kh * n_pairs + pp) * PAIR:(kh * n_pairs + pp + 1) * PAIR, :] for pp in range(n_pairs)],
            axis=1)
        c0 = (2 * kh + par) * PAIR
        s_ctx = _dot(kc_ref[:, c0:c0 + PAIR], qt)
        s_band = _dot(kl_ref[pl.ds(s0, BAND), c0:c0 + PAIR], qt) + bias_ref[0] if with_band else None
        return s_ctx, s_band

    def weighted_values(kh, par, s_ctx, s_band):
        c0 = (2 * kh + par) * PAIR
        sink = jnp.concatenate(
            [jnp.full((1, ATT_BLOCK), sink_ref[layer, kh * KV_GROUP + 2 * pp + par] * LOG2E, f32)
             for pp in range(n_pairs)], axis=1)
        m = jnp.maximum(jnp.max(s_ctx, axis=0, keepdims=True), sink)
        if with_band:
            m = jnp.maximum(m, jnp.max(s_band, axis=0, keepdims=True))
        vct = jnp.concatenate(
            [vct_ref[r, c0:c0 + PAIR, :] for r in range(CTX_LEN // ATT_BLOCK)], axis=1)
        o = _dot(vct, jnp.exp2(s_ctx - m).astype(bf16))
        if with_band:
            vlt = jnp.concatenate(
                [vlt_ref[blk0 + r, c0:c0 + PAIR, :] for r in range(BAND // ATT_BLOCK)], axis=1)
            o = o + _dot(vlt, jnp.exp2(s_band - m).astype(bf16))
        den_row = HEAD_DIM if par == 0 else 0
        den = o[den_row:den_row + 1, :] + jnp.exp2(sink - m)
        return o * (1.0 / den)

    order = [(kh, par) for kh in range(N_KV_HEADS) for par in range(2)]
    pending = [scores(*order[it]) for it in range(ATTN_LOOKAHEAD)]
    halves = []
    for it, (kh, par) in enumerate(order):
        if it + ATTN_LOOKAHEAD < len(order):
            pending.append(scores(*order[it + ATTN_LOOKAHEAD]))
        halves.append(weighted_values(kh, par, *pending.pop(0)))
        if par == 1:
            acc = jnp.where(even_rows, halves[0], halves[1])
            halves = []
            for pp in range(n_pairs):
                c0 = (kh * n_pairs + pp) * PAIR
                o_ref[:, c0:c0 + PAIR] = acc[:, pp * ATT_BLOCK:(pp + 1) * ATT_BLOCK].T.astype(bf16)


def _band_bias():
    kj = np.arange(BAND)[:, None]
    qi = np.arange(KV_GROUP // 2 * ATT_BLOCK)[None, :] % ATT_BLOCK
    tabs = [np.where(np.abs(qi + rel - kj) <= WINDOW, 0.0, NEG_INF) for rel in (0, ATT_BLOCK, 2 * ATT_BLOCK)]
    return jnp.asarray(np.stack(tabs), f32)


def _attention(sink, qt, k_ext, vt_ext, j, with_ctx_rows):
    ctx_blk = T_LAT // CTX_LEN
    ctx_sub = CTX_LEN // ATT_BLOCK
    nb = SEQ // ATT_BLOCK
    ctx_steps = ctx_sub if with_ctx_rows else 0
    tok_blk = lambda b, n: jnp.where(n < nb, b * nb + n, T_LAT // ATT_BLOCK + b * ctx_sub + n - nb)
    bias_row = lambda b, n: jnp.where(n == 0, 0, jnp.where(n == nb - 1, 2, 1))
    return pl.pallas_call(
        functools.partial(_attn_kernel, layer=j, ctx_steps=ctx_steps),
        grid=(BATCH, nb + ctx_steps),
        in_specs=[
            pl.BlockSpec(memory_space=pltpu.SMEM),
            pl.BlockSpec((1, D, ATT_BLOCK), lambda b, n: (tok_blk(b, n), 0, 0)),
            pl.BlockSpec((CTX_LEN, KV_EXT), lambda b, n: (ctx_blk + b, 0)),
            pl.BlockSpec((ctx_sub, KV_EXT, ATT_BLOCK), lambda b, n: (ctx_blk + b, 0, 0)),
            pl.BlockSpec((1, BAND, KV_GROUP // 2 * ATT_BLOCK), lambda b, n: (bias_row(b, n), 0, 0)),
            pl.BlockSpec((SEQ, KV_EXT), lambda b, n: (b, 0)),
            pl.BlockSpec((nb, KV_EXT, ATT_BLOCK), lambda b, n: (b, 0, 0)),
        ],
        out_specs=pl.BlockSpec((ATT_BLOCK, D), lambda b, n: (tok_blk(b, n), 0)),
        out_shape=jax.ShapeDtypeStruct((T_ALL if with_ctx_rows else T_LAT, D), bf16),
        compiler_params=_params(2),
        name="attention",
    )(sink, qt, k_ext, vt_ext, _band_bias(), k_ext, vt_ext)


def _proj_kernel(x_ref, o_ref_in, mod_ref, ng_ref, w_ref, out_ref):
    n_sub = x_ref.shape[0] // MIX_SUB
    ys = [_dot(o_ref_in[s * MIX_SUB:(s + 1) * MIX_SUB, :], w_ref[0]) for s in range(n_sub)]
    for s in range(n_sub):
        rows = slice(s * MIX_SUB, (s + 1) * MIX_SUB)
        out_ref[rows, :] = x_ref[rows, :] + mod_ref[0, 0][2:3] * _rms(ys[s], ng_ref[0][1:2])


def _proj_layer(x_all, o_all, mod, norm_g, w_o, i, j, n_tok):
    tm = TM_MIX
    row = _mod_row(tm)
    return pl.pallas_call(
        _proj_kernel,
        grid=(n_tok // tm,),
        in_specs=[
            pl.BlockSpec((tm, D), lambda t: (t, 0)),
            pl.BlockSpec((tm, D), lambda t: (t, 0)),
            pl.BlockSpec((1, 1, N_MOD, D), lambda t: (i, row(t), 0, 0)),
            pl.BlockSpec((1, 4, D), lambda t: (i, 0, 0)),
            _resident((1, D, D), lambda t: (j, 0, 0)),
        ],
        out_specs=pl.BlockSpec((tm, D), lambda t: (t, 0)),
        out_shape=jax.ShapeDtypeStruct((n_tok, D), f32),
        compiler_params=_params(1),
        name="attn_proj",
    )(x_all, o_all, mod, norm_g, w_o)


def _row_loop(n_rows, body):
    def step(r, carry):
        body(pl.ds(pl.multiple_of(r * ROW_CHUNK, ROW_CHUNK), ROW_CHUNK))
        return carry
    lax.fori_loop(0, n_rows // ROW_CHUNK, step, 0, unroll=ROW_UNROLL)


def _ffn_kernel(*refs, n_cast):
    x_ref, mod_ref, ng_ref, wg_ref, wu_ref, wd_ref = refs[:6]
    o_ref = refs[6 + n_cast]
    h_scr, acc_scr = refs[-2:]
    hstep = pl.program_id(1)
    tm = x_ref.shape[0]
    for src_ref, dst_ref in zip(refs[6:6 + n_cast], refs[7 + n_cast:7 + 2 * n_cast]):
        dst_ref[...] = src_ref[...].astype(bf16)

    @pl.when(hstep == 0)
    def _():
        mod = mod_ref[0, 0]
        gain = ng_ref[0][2:3] * (1.0 + mod[4:5])
        shift = mod[3:4]

        def body(rows):
            xr = x_ref[rows, :]
            inv = lax.rsqrt(jnp.mean(xr * xr, axis=-1, keepdims=True) + EPS)
            h_scr[rows, :] = (xr * inv * gain + shift).astype(bf16)
            acc_scr[rows, :] = jnp.zeros((ROW_CHUNK, D), f32)
        _row_loop(tm, body)

    h = h_scr[...]
    g = _dot(h, wg_ref[0].astype(bf16))
    u = _dot(h, wu_ref[0].astype(bf16))
    a = (g * jax.nn.sigmoid(g) * u).astype(bf16)
    wd = wd_ref[0].astype(bf16)
    for c in range(D // FFN_DOWN_COLS):
        cols = slice(c * FFN_DOWN_COLS, (c + 1) * FFN_DOWN_COLS)
        acc_scr[:, cols] += _dot(a, wd[:, cols])

    @pl.when(hstep == pl.num_programs(1) - 1)
    def _():
        gain = mod_ref[0, 0][5:6] * ng_ref[0][3:4]

        def body(rows):
            y = acc_scr[rows, :]
            inv = lax.rsqrt(jnp.mean(y * y, axis=-1, keepdims=True) + EPS)
            o_ref[rows, :] = x_ref[rows, :] + y * inv * gain
        _row_loop(tm, body)


def _ffn_layer(x_all, mod, norm_g, w_gu, w_down, i, n_tok, cast=()):
    tm, th = TM_FFN, TH_FFN
    row = _mod_row(tm)
    nh = FFN_HIDDEN // th
    n_tiles = n_tok // tm
    assert not cast or n_tiles * nh >= CAST_STEPS
    cast_blk = lambda t, h: jnp.minimum(t * nh + h, CAST_STEPS - 1)
    cast_in, cast_out, cast_shapes = [], [], []
    for w, layer in cast:
        _, rows, cols = w.shape
        blk = (1, rows // CAST_STEPS, cols)
        cast_in.append(pl.BlockSpec(blk, lambda t, h, layer=layer: (layer, cast_blk(t, h), 0)))
        cast_out.append(pl.BlockSpec(blk, lambda t, h: (0, cast_blk(t, h), 0)))
        cast_shapes.append(jax.ShapeDtypeStruct((1, rows, cols), bf16))
    outs = pl.pallas_call(
        functools.partial(_ffn_kernel, n_cast=len(cast)),
        grid=(n_tiles, nh),
        in_specs=[
            pl.BlockSpec((tm, D), lambda t, h: (t, 0)),
            pl.BlockSpec((1, 1, N_MOD, D), lambda t, h: (i, row(t), 0, 0)),
            pl.BlockSpec((1, 4, D), lambda t, h: (i, 0, 0)),
            pl.BlockSpec((1, D, th), lambda t, h: (i, 0, h)),
            pl.BlockSpec((1, D, th), lambda t, h: (i, 0, nh + h)),
            pl.BlockSpec((1, th, D), lambda t, h: (i, h, 0)),
        ] + cast_in,
        out_specs=[pl.BlockSpec((tm, D), lambda t, h: (t, 0))] + cast_out,
        out_shape=[jax.ShapeDtypeStruct((n_tok, D), f32)] + cast_shapes,
        scratch_shapes=[pltpu.VMEM((tm, D), bf16), pltpu.VMEM((tm, D), f32)],
        compiler_params=_params(2, BIG_VMEM_LIMIT),
        name="swiglu_ffn",
    )(x_all, mod, norm_g, w_gu, w_gu, w_down, *[w for w, _ in cast])
    return outs[0], tuple(outs[1:])


def _rope_tables():
    rows = SEQ // GRID_W
    pos = np.stack([np.repeat(np.arange(rows), GRID_W), np.tile(np.arange(GRID_W), rows)], -1)
    n_freq = HEAD_DIM // 4
    inv = jnp.asarray(ROPE_THETA, f32) ** (-jnp.arange(n_freq, dtype=f32) / n_freq)
    ang = jnp.asarray(pos, f32)[:, :, None] * inv
    cos, sin = jnp.cos(ang), jnp.sin(ang)
    zero = jnp.zeros_like(sin)
    cos_h = jnp.concatenate([cos[:, 0], cos[:, 0], cos[:, 1], cos[:, 1]], -1)
    sa_h = jnp.concatenate([zero[:, 0], sin[:, 0], zero[:, 1], sin[:, 1]], -1)
    sb_h = jnp.concatenate([-sin[:, 0], zero[:, 0], -sin[:, 1], zero[:, 1]], -1)
    ident = (jnp.ones((TM_MIX, HEAD_DIM), f32), jnp.zeros((TM_MIX, HEAD_DIM), f32),
             jnp.zeros((TM_MIX, HEAD_DIM), f32))
    return tuple(jnp.tile(jnp.concatenate([t, e], 0), (1, LANES // HEAD_DIM))
                 for t, e in zip((cos_h, sa_h, sb_h), ident))


def kernel(x, c, ctx, c_ctx, w_mod, b_mod, norm_g, a_w_in, a_v_norm_g, a_w_s, a_b_s, a_w_out,
           b_w_qkv, b_sink, b_w_o, f_w_gu, f_w_down):
    assert x.shape == (BATCH, SEQ, D) and ctx.shape == (BATCH, CTX_LEN, D)
    xs = (x.reshape(T_LAT, D), ctx.reshape(T_CTX, D))
    c_all = jnp.concatenate(
        [c, c_ctx[None], jnp.zeros((MOD_ROWS - BATCH - 1, D), f32)], axis=0)
    mod = _modulation(c_all, w_mod, b_mod).reshape(DEPTH, MOD_ROWS, N_MOD, D)

    w_s = a_w_s.astype(bf16)
    b_full = jnp.repeat(jnp.swapaxes(a_b_s, 1, 2), GROUP_DIM, axis=2)
    v_g = a_v_norm_g.reshape(-1, 1, D)
    rope = _rope_tables()

    mixer_w = (a_w_in[:1].astype(bf16), a_w_out[:1].astype(bf16))
    for i in range(DEPTH):
        last = i == DEPTH - 1
        j = i // 2
        n_tok = T_LAT if last else T_ALL
        if i % 2 == 0:
            x_all = _gmlp_layer(xs if i == 0 else (x_all,), mod, norm_g, mixer_w[0], v_g[j:j + 1], w_s[j:j + 1],
                                b_full[j:j + 1], mixer_w[1], i, 0, n_tok)
        else:
            qt, k_ext, vt_ext = _qkv_layer(x_all, mod, norm_g, mixer_w[0], rope, i, 0)
            o = _attention(b_sink, qt, k_ext, vt_ext, j, with_ctx_rows=not last)
            x_all = _proj_layer(x_all, o, mod, norm_g, mixer_w[1], i, 0, n_tok)
        if last:
            cast = ()
        elif i % 2 == 0:
            cast = ((b_w_qkv, j), (b_w_o, j))
        else:
            cast = ((a_w_in, j + 1), (a_w_out, j + 1))
        x_all, mixer_w = _ffn_layer(x_all, mod, norm_g, f_w_gu, f_w_down, i, n_tok, cast)
    return x_all.reshape(BATCH, SEQ, D)
```
